```python
import jax, jax.numpy as jnp
from jax import lax
import numpy as np

D_MODEL = 1024
BATCH = 8
SEQ = 2048
DEPTH = 4
DEC_BATCH = 128
DEC_SEQ = 1
PAST_LEN = 8192
PAGE_SIZE = 128

N_EVEN = (DEPTH + 1) // 2
N_ODD = DEPTH // 2
N_MOD = 6
EPS = 1e-6
NEG = -1e30
CONV_CH = D_MODEL // 2
CONV_W = 3
HG_WIDTH = D_MODEL // 2
HG_DK = 128
HG_HEADS = HG_WIDTH // HG_DK
HG_CHUNK = 64
EVEN_SPLITS = (CONV_CH, 2 * CONV_CH, 3 * CONV_CH, 3 * CONV_CH + HG_WIDTH,
               3 * CONV_CH + 2 * HG_WIDTH, 3 * CONV_CH + 3 * HG_WIDTH)
EVEN_IN = 3 * CONV_CH + 4 * HG_WIDTH
EVEN_OUT = CONV_CH + HG_WIDTH
MLA_HEADS = 8
NOPE = 128
ROPE_D = 64
V_D = 128
Q_LORA = 384
KV_LORA = 256
ROPE_THETA = 10000.0
MLA_SCALE = (NOPE + ROPE_D) ** -0.5
Q_BLOCK = 128
D_FF = ((8 * D_MODEL // 3 + 255) // 256) * 256

kernel_name = 'hybrid_conv_hgrn2_mla_adaln_decode_step'

F32 = jnp.float32


def rmsnorm(x, g):
    xf = x.astype(F32)
    y = xf * lax.rsqrt(jnp.mean(xf * xf, axis=-1, keepdims=True) + EPS)
    return (y * g.astype(F32)).astype(x.dtype)


def adaln_mod(c, w, b):
    m = jax.nn.silu(c) @ w + b
    return [t[:, None, :] for t in jnp.split(m, N_MOD, axis=-1)]


def modulate(x, g, shift, scale):
    return rmsnorm(x, g) * (1 + scale) + shift


def rope(x, pos):
    half = ROPE_D // 2
    inv = ROPE_THETA ** (-jnp.arange(half, dtype=F32) / half)
    ang = pos.astype(F32)[:, None] * inv[None, :]
    cos = jnp.cos(ang)[None, :, None, :]
    sin = jnp.sin(ang)[None, :, None, :]
    x1 = x[..., :half].astype(F32)
    x2 = x[..., half:].astype(F32)
    return jnp.concatenate([x1 * cos - x2 * sin, x1 * sin + x2 * cos], axis=-1).astype(x.dtype)


def short_conv(u, buf, w):
    T = u.shape[1]
    p = jnp.concatenate([buf.astype(u.dtype), u], axis=1)
    y = w[0] * p[:, 0:T]
    for j in range(1, CONV_W):
        y = y + w[j] * p[:, j:j + T]
    return y, p[:, -(CONV_W - 1):]


def hgrn_gates(q_pre, f_pre, lb):
    N, T, _ = q_pre.shape
    fp = f_pre.astype(F32)
    sig = jax.nn.sigmoid(fp)
    logf = jnp.log(lb + (1 - lb) * sig)
    k = (1 - lb) * jax.nn.sigmoid(-fp)
    q = jax.nn.silu(q_pre.astype(F32)) * (HG_DK ** -0.5)
    shp = (N, T, HG_HEADS, HG_DK)
    return q.reshape(shp), k.reshape(shp), logf.reshape(shp)


def hgrn_chunked(q, k, v, logf):
    N, T, H, K = q.shape
    n = T // HG_CHUNK

    def chunks(a):
        return a.reshape(N, n, HG_CHUNK, H, a.shape[-1]).transpose(1, 0, 3, 2, 4)

    causal = jnp.tril(jnp.ones((HG_CHUNK, HG_CHUNK), bool))[:, :, None]

    def step(S, inp):
        qc, kc, vc, gc = inp
        G = jnp.cumsum(gc, axis=2)
        o = jnp.einsum('nhtk,nhkv->nhtv', qc * jnp.exp(G), S)
        decay = jnp.exp(jnp.where(causal, G[:, :, :, None] - G[:, :, None], NEG))
        A = jnp.einsum('nhtk,nhsk,nhtsk->nhts', qc, kc, decay)
        o = o + jnp.einsum('nhts,nhsv->nhtv', A, vc)
        GL = G[:, :, -1:]
        S = jnp.exp(GL[:, :, 0])[..., None] * S + jnp.einsum('nhsk,nhsv->nhkv', kc * jnp.exp(GL - G), vc)
        return S, o

    S0 = jnp.zeros((N, H, K, v.shape[-1]), F32)
    S, o = lax.scan(step, S0, (chunks(q), chunks(k), chunks(v), chunks(logf)))
    o = o.transpose(1, 0, 3, 2, 4).reshape(N, T, H, v.shape[-1])
    return o, S


def hgrn_recurrent(q, k, v, logf, S0):
    def step(S, inp):
        qt, kt, vt, gt = inp
        S = jnp.exp(gt)[..., None] * S + kt[..., None] * vt[..., None, :]
        return S, jnp.einsum('nhk,nhkv->nhv', qt, S)

    tm = lambda a: jnp.moveaxis(a, 1, 0)
    S, o = lax.scan(step, S0.astype(F32), (tm(q), tm(k), tm(v), tm(logf)))
    return jnp.moveaxis(o, 0, 1), S


def even_mixer(h, conv_buf, S0, w_in, conv_w, lb, gnorm, w_out):
    N, T, _ = h.shape
    z = h @ w_in
    bg, cg, xv, qp, fp, ip, gp = jnp.split(z, EVEN_SPLITS, axis=-1)
    conv_y, new_buf = short_conv(cg * xv, conv_buf, conv_w)
    a_out = bg * conv_y
    q, k, logf = hgrn_gates(qp, fp, lb)
    v = ip.astype(F32).reshape(N, T, HG_HEADS, HG_DK)
    if S0 is None:
        o, S = hgrn_chunked(q, k, v, logf)
    else:
        o, S = hgrn_recurrent(q, k, v, logf, S0)
    o = rmsnorm(o, gnorm) * jax.nn.silu(gp.astype(F32).reshape(N, T, HG_HEADS, HG_DK))
    b_out = o.reshape(N, T, HG_WIDTH).astype(h.dtype)
    y = jnp.concatenate([a_out, b_out], axis=-1) @ w_out
    return y, new_buf, S


def mla_project(h, pos, w_dqkv, q_norm, w_uq, kv_norm, w_uk):
    N, T, _ = h.shape
    d = h @ w_dqkv
    cq = rmsnorm(d[..., :Q_LORA], q_norm)
    ckv = rmsnorm(d[..., Q_LORA:Q_LORA + KV_LORA], kv_norm)
    kr = rope(d[..., Q_LORA + KV_LORA:][:, :, None], pos)[:, :, 0]
    q = (cq @ w_uq).reshape(N, T, MLA_HEADS, NOPE + ROPE_D)
    q_lat = jnp.einsum('nthd,chd->nthc', q[..., :NOPE], w_uk)
    q_rope = rope(q[..., NOPE:], pos)
    return q_lat, q_rope, ckv, kr


def latent_scores(q_lat, q_rope, ckv, kr):
    s = jnp.einsum('nthc,nsc->nhts', q_lat, ckv) + jnp.einsum('nthr,nsr->nhts', q_rope, kr)
    return s.astype(F32) * MLA_SCALE


def mla_out(ctx, w_uv, w_o):
    N, T = ctx.shape[:2]
    o = jnp.einsum('nthc,chv->nthv', ctx, w_uv)
    return o.reshape(N, T, MLA_HEADS * V_D) @ w_o


def mla_prompt_attend(q_lat, q_rope, ckv, kr):
    T = q_lat.shape[1]
    outs = []
    for i in range(T // Q_BLOCK):
        lo, hi = i * Q_BLOCK, (i + 1) * Q_BLOCK
        s = latent_scores(q_lat[:, lo:hi], q_rope[:, lo:hi], ckv[:, :hi], kr[:, :hi])
        mask = jnp.arange(hi)[None, :] <= jnp.arange(lo, hi)[:, None]
        p = jax.nn.softmax(jnp.where(mask, s, NEG), axis=-1).astype(ckv.dtype)
        outs.append(jnp.einsum('nhts,nsc->nthc', p, ckv[:, :hi]))
    return jnp.concatenate(outs, axis=1)


def mla_sample_attend(q_lat, q_rope, ckv_new, kr_new, ckv_past, kr_past):
    T = q_lat.shape[1]
    P = ckv_past.shape[1]
    s_past = latent_scores(q_lat, q_rope, ckv_past, kr_past)
    s_new = latent_scores(q_lat, q_rope, ckv_new, kr_new)
    s_new = jnp.where(jnp.tril(jnp.ones((T, T), bool)), s_new, NEG)
    p = jax.nn.softmax(jnp.concatenate([s_past, s_new], axis=-1), axis=-1).astype(ckv_new.dtype)
    return (jnp.einsum('nhts,nsc->nthc', p[..., :P], ckv_past)
            + jnp.einsum('nhts,nsc->nthc', p[..., P:], ckv_new))


def swiglu(h, w_gu, w_down):
    g, u = jnp.split(h @ w_gu, 2, axis=-1)
    return (jax.nn.silu(g) * u) @ w_down


def setup_inputs(seed: int = 0) -> dict:
    key = jax.random.key(seed)
    ks = iter(jax.random.split(key, 40))
    nrm = lambda shape, scale: jax.random.normal(next(ks), shape, F32) * scale
    D = D_MODEL
    n_pages = PAST_LEN // PAGE_SIZE
    n_used = DEC_BATCH * n_pages
    n_phys = n_used + (n_used + 3) // 4
    page_table = jax.random.permutation(next(ks), n_phys)[:n_used].reshape(DEC_BATCH, n_pages).astype(jnp.int32)
    return {
        'x_prompt': nrm((BATCH, SEQ, D), 1.0),
        'x_sample': nrm((DEC_BATCH, DEC_SEQ, D), 1.0),
        'c_prompt': nrm((BATCH, D), 1.0),
        'c_sample': nrm((DEC_BATCH, D), 1.0),
        'cache_conv': nrm((N_EVEN, DEC_BATCH, CONV_W - 1, CONV_CH), 1.0),
        'state_hgrn': nrm((N_EVEN, DEC_BATCH, HG_HEADS, HG_DK, HG_DK), 0.5),
        'cache_ckv': nrm((N_ODD, n_phys, PAGE_SIZE, KV_LORA), 1.0),
        'cache_krope': nrm((N_ODD, n_phys, PAGE_SIZE, ROPE_D), 1.0),
        'page_table': page_table,
        'w_ada': nrm((DEPTH, D, N_MOD * D), 0.5 * D ** -0.5),
        'b_ada': nrm((DEPTH, N_MOD * D), 0.01),
        'norm_mix': 1.0 + nrm((DEPTH, D), 0.02),
        'norm_ffn': 1.0 + nrm((DEPTH, D), 0.02),
        'w_in_even': nrm((N_EVEN, D, EVEN_IN), D ** -0.5),
        'conv_w': nrm((N_EVEN, CONV_W, CONV_CH), CONV_W ** -0.5),
        'hg_lb_logits': nrm((N_EVEN, HG_WIDTH), 0.5),
        'hg_gnorm': 1.0 + nrm((N_EVEN, HG_DK), 0.02),
        'w_out_even': nrm((N_EVEN, EVEN_OUT, D), EVEN_OUT ** -0.5),
        'w_dqkv': nrm((N_ODD, D, Q_LORA + KV_LORA + ROPE_D), D ** -0.5),
        'q_norm': 1.0 + nrm((N_ODD, Q_LORA), 0.02),
        'w_uq': nrm((N_ODD, Q_LORA, MLA_HEADS * (NOPE + ROPE_D)), Q_LORA ** -0.5),
        'kv_norm': 1.0 + nrm((N_ODD, KV_LORA), 0.02),
        'w_uk': nrm((N_ODD, KV_LORA, MLA_HEADS, NOPE), KV_LORA ** -0.5),
        'w_uv': nrm((N_ODD, KV_LORA, MLA_HEADS, V_D), KV_LORA ** -0.5),
        'w_o': nrm((N_ODD, MLA_HEADS * V_D, D), (MLA_HEADS * V_D) ** -0.5),
        'w_gu': nrm((DEPTH, D, 2 * D_FF), D ** -0.5),
        'w_down': nrm((DEPTH, D_FF, D), D_FF ** -0.5),
        'norm_final': 1.0 + nrm((D,), 0.02),
    }


def reference(x_prompt, x_sample, c_prompt, c_sample, cache_conv, state_hgrn, cache_ckv, cache_krope,
              page_table, w_ada, b_ada, norm_mix, norm_ffn, w_in_even, conv_w, hg_lb_logits, hg_gnorm,
              w_out_even, w_dqkv, q_norm, w_uq, kv_norm, w_uk, w_uv, w_o, w_gu, w_down, norm_final):
    Bp, Tp, _ = x_prompt.shape
    Bs, Ts, _ = x_sample.shape
    past_len = page_table.shape[1] * PAGE_SIZE
    pos_p = jnp.arange(Tp)
    pos_s = past_len + jnp.arange(Ts)
    lb_sm = jax.nn.softmax(hg_lb_logits.astype(F32), axis=0)
    lb_all = jnp.concatenate([jnp.zeros_like(lb_sm[:1]), jnp.cumsum(lb_sm, axis=0)[:-1]], axis=0)

    xp, xs = x_prompt, x_sample
    conv_p, hg_p, ckv_p, kr_p = [], [], [], []
    conv_s, hg_s, ckv_s, kr_s = [], [], [], []
    for l in range(DEPTH):
        j = l // 2
        mp = adaln_mod(c_prompt, w_ada[l], b_ada[l])
        ms = adaln_mod(c_sample, w_ada[l], b_ada[l])
        hp = modulate(xp, norm_mix[l], mp[0], mp[1])
        hs = modulate(xs, norm_mix[l], ms[0], ms[1])
        if l % 2 == 0:
            zero_buf = jnp.zeros((Bp, CONV_W - 1, CONV_CH), hp.dtype)
            yp, bp_, Sp = even_mixer(hp, zero_buf, None, w_in_even[j], conv_w[j], lb_all[j],
                                     hg_gnorm[j], w_out_even[j])
            ys, bs_, Ss = even_mixer(hs, cache_conv[j], state_hgrn[j], w_in_even[j], conv_w[j],
                                     lb_all[j], hg_gnorm[j], w_out_even[j])
            conv_p.append(bp_)
            hg_p.append(Sp)
            conv_s.append(bs_)
            hg_s.append(Ss)
        else:
            qlp, qrp, ckvp, krp = mla_project(hp, pos_p, w_dqkv[j], q_norm[j], w_uq[j], kv_norm[j], w_uk[j])
            yp = mla_out(mla_prompt_attend(qlp, qrp, ckvp, krp), w_uv[j], w_o[j])
            qls, qrs, ckvs, krs = mla_project(hs, pos_s, w_dqkv[j], q_norm[j], w_uq[j], kv_norm[j], w_uk[j])
            ckv_past = cache_ckv[j, page_table].reshape(Bs, past_len, KV_LORA)
            kr_past = cache_krope[j, page_table].reshape(Bs, past_len, ROPE_D)
            ys = mla_out(mla_sample_attend(qls, qrs, ckvs, krs, ckv_past, kr_past), w_uv[j], w_o[j])
            ckv_p.append(ckvp)
            kr_p.append(krp)
            ckv_s.append(ckvs)
            kr_s.append(krs)
        xp = xp + mp[2] * yp
        xs = xs + ms[2] * ys
        hp = modulate(xp, norm_ffn[l], mp[3], mp[4])
        hs = modulate(xs, norm_ffn[l], ms[3], ms[4])
        xp = xp + mp[5] * swiglu(hp, w_gu[l], w_down[l])
        xs = xs + ms[5] * swiglu(hs, w_gu[l], w_down[l])

    y_prompt = rmsnorm(xp, norm_final)
    y_sample = rmsnorm(xs, norm_final)
    return (y_prompt, y_sample, jnp.stack(conv_p), jnp.stack(hg_p), jnp.stack(ckv_p), jnp.stack(kr_p),
            jnp.stack(conv_s), jnp.stack(hg_s), jnp.stack(ckv_s), jnp.stack(kr_s))
```

```python
import functools
import math

import jax
import jax.numpy as jnp
from jax import lax
from jax.experimental import pallas as pl
from jax.experimental.pallas import tpu as pltpu

F32 = jnp.float32
BF16 = jnp.bfloat16

D_MODEL = 1024
N_MOD = 6
EPS = 1e-6
NEG = -1e30
PAGE_SIZE = 128
CONV_CH = D_MODEL // 2
CONV_W = 3
HG_WIDTH = D_MODEL // 2
HG_DK = 128
HG_HEADS = HG_WIDTH // HG_DK
EVEN_IN = 3 * CONV_CH + 4 * HG_WIDTH
MLA_HEADS = 8
NOPE = 128
ROPE_D = 64
V_D = 128
Q_LORA = 384
KV_LORA = 256
ROPE_THETA = 10000.0
MLA_SCALE = (NOPE + ROPE_D) ** -0.5
LANES = 128

Z_B, Z_C, Z_X = 0, CONV_CH, 2 * CONV_CH
Z_Q = 3 * CONV_CH
Z_F = Z_Q + HG_WIDTH
Z_I = Z_F + HG_WIDTH
Z_G = Z_I + HG_WIDTH

HG_SUB = 16
VMEM_LIMIT = 56 * 1024 * 1024


def _cp(sem, vmem=VMEM_LIMIT):
    return pltpu.CompilerParams(dimension_semantics=sem, vmem_limit_bytes=vmem)


def _silu(x):
    return x * jax.nn.sigmoid(x)


def _rmsnorm(x, g):
    return x * lax.rsqrt(jnp.mean(x * x, axis=-1, keepdims=True) + EPS) * g


def _dot(a, b):
    return jnp.dot(a, b, preferred_element_type=F32)


def _dot_nt(a, b):
    return lax.dot_general(a, b, (((1,), (1,)), ((), ())), preferred_element_type=F32)


def _dot_tn(a, b):
    return lax.dot_general(a, b, (((0,), (0,)), ((), ())), preferred_element_type=F32)


def _ada_kernel(c_ref, w_ref, b_ref, o_ref):
    a = _silu(c_ref[...]).astype(BF16)
    o_ref[...] = _dot(a, w_ref[...].astype(BF16)) + b_ref[...]


def _ada_all(c_all, w_ada, b_ada, tn=1536):
    depth, d, n6 = w_ada.shape
    rows = c_all.shape[0]
    return pl.pallas_call(
        _ada_kernel,
        grid=(depth, n6 // tn),
        in_specs=[
            pl.BlockSpec((rows, d), lambda l, j: (0, 0)),
            pl.BlockSpec((None, d, tn), lambda l, j: (l, 0, j)),
            pl.BlockSpec((None, 1, tn), lambda l, j: (l, 0, j)),
        ],
        out_specs=pl.BlockSpec((None, rows, tn), lambda l, j: (l, 0, j)),
        out_shape=jax.ShapeDtypeStruct((depth, rows, n6), F32),
        compiler_params=_cp(("parallel", "parallel")),
        name="adaln_mod",
    )(c_all, w_ada, b_ada.reshape(depth, 1, n6))


class _Mod:
    def __init__(self, arr, tps):
        self.arr = arr
        self.tps = tps
        self.r = arr.shape[2]

    def spec(self, layer, col, width=D_MODEL, ncol=None):
        tps = self.tps
        per = D_MODEL // width
        if ncol is None:
            return pl.BlockSpec((None, None, self.r, width),
                                lambda i, *_: (layer, i // tps, 0, col * per))
        return pl.BlockSpec((None, None, self.r, width),
                            lambda i, j, *_: (layer, i // tps, 0, col * per + j))


def _modmm_kernel(x_ref, g_ref, sh_ref, sc_ref, w_ref, o_ref, h_scr):
    @pl.when(pl.program_id(1) == 0)
    def _():
        h = _rmsnorm(x_ref[...], g_ref[...]) * (1.0 + sc_ref[...]) + sh_ref[...]
        h_scr[...] = h.astype(BF16)

    o_ref[...] = _dot(h_scr[...], w_ref[...]).astype(o_ref.dtype)


def _mod_matmul(x, g, mod, layer, w, tm, tn, out_dtype=F32):
    m, d = x.shape
    n = w.shape[1]
    return pl.pallas_call(
        _modmm_kernel,
        grid=(m // tm, n // tn),
        in_specs=[
            pl.BlockSpec((tm, d), lambda i, j: (i, 0)),
            pl.BlockSpec((1, d), lambda i, j: (0, 0)),
            mod.spec(layer, 0),
            mod.spec(layer, 1),
            pl.BlockSpec((d, tn), lambda i, j: (0, j)),
        ],
        out_specs=pl.BlockSpec((tm, tn), lambda i, j: (i, j)),
        out_shape=jax.ShapeDtypeStruct((m, n), out_dtype),
        scratch_shapes=[pltpu.VMEM((tm, d), BF16)],
        compiler_params=_cp(("parallel", "arbitrary")),
        name="mod_matmul",
    )(x, g, mod.arr, mod.arr, w)


def _resmm_kernel(a_ref, w_ref, x_ref, gate_ref, o_ref):
    o_ref[...] = x_ref[...] + gate_ref[...] * _dot(a_ref[...], w_ref[...])


def _res_matmul(a, w, x, mod, layer, gate_col, tm, tn):
    m, k = a.shape
    n = w.shape[1]
    return pl.pallas_call(
        _resmm_kernel,
        grid=(m // tm, n // tn),
        in_specs=[
            pl.BlockSpec((tm, k), lambda i, j: (i, 0)),
            pl.BlockSpec((k, tn), lambda i, j: (0, j)),
            pl.BlockSpec((tm, tn), lambda i, j: (i, j)),
            mod.spec(layer, gate_col, width=tn, ncol=True),
        ],
        out_specs=pl.BlockSpec((tm, tn), lambda i, j: (i, j)),
        out_shape=jax.ShapeDtypeStruct((m, n), F32),
        compiler_params=_cp(("parallel", "parallel")),
        name="res_matmul",
    )(a, w, x, mod.arr)


def _ffn_kernel(x_ref, g_ref, sh_ref, sc_ref, gate_ref, wg_ref, wu_ref, wd_ref, gf_ref, o_ref,
                h_scr, acc_scr, *, final_norm):
    f = pl.program_id(1)

    @pl.when(f == 0)
    def _():
        h = _rmsnorm(x_ref[...], g_ref[...]) * (1.0 + sc_ref[...]) + sh_ref[...]
        h_scr[...] = h.astype(BF16)
        acc_scr[...] = jnp.zeros_like(acc_scr)

    h = h_scr[...]
    gg = _dot(h, wg_ref[...])
    uu = _dot(h, wu_ref[...])
    a = (_silu(gg) * uu).astype(BF16)
    acc_scr[...] += _dot(a, wd_ref[...])

    @pl.when(f == pl.num_programs(1) - 1)
    def _():
        y = x_ref[...] + gate_ref[...] * acc_scr[...]
        if final_norm:
            y = _rmsnorm(y, gf_ref[...])
        o_ref[...] = y


def _ffn(x, g, mod, layer, w_gu, w_down, g_final, final_norm, tm, tf):
    m, d = x.shape
    dff = w_down.shape[0]
    nf = dff // tf
    return pl.pallas_call(
        functools.partial(_ffn_kernel, final_norm=final_norm),
        grid=(m // tm, nf),
        in_specs=[
            pl.BlockSpec((tm, d), lambda i, f: (i, 0)),
            pl.BlockSpec((1, d), lambda i, f: (0, 0)),
            mod.spec(layer, 3),
            mod.spec(layer, 4),
            mod.spec(layer, 5),
            pl.BlockSpec((d, tf), lambda i, f: (0, f)),
            pl.BlockSpec((d, tf), lambda i, f: (0, nf + f)),
            pl.BlockSpec((tf, d), lambda i, f: (f, 0)),
            pl.BlockSpec((1, d), lambda i, f: (0, 0)),
        ],
        out_specs=pl.BlockSpec((tm, d), lambda i, f: (i, 0)),
        out_shape=jax.ShapeDtypeStruct((m, d), F32),
        scratch_shapes=[pltpu.VMEM((tm, d), BF16), pltpu.VMEM((tm, d), F32)],
        compiler_params=_cp(("parallel", "arbitrary")),
        name="ffn",
    )(x, g, mod.arr, mod.arr, mod.arr, w_gu, w_gu, w_down, g_final)


def _hg_lower_bound(lbl_ref, j):
    logits = lbl_ref[...]
    e = jnp.exp(logits - jnp.max(logits, axis=0, keepdims=True))
    den = jnp.sum(e, axis=0, keepdims=True)
    lb = jnp.zeros_like(den)
    for i in range(j):
        lb = lb + e[i:i + 1, :] / den
    return lb


def _cumsum_rows(x):
    rows = x.shape[0]
    idx = lax.broadcasted_iota(jnp.int32, x.shape, 0)
    d = 1
    while d < rows:
        x = x + jnp.where(idx >= d, pltpu.roll(x, d, 0), 0.0)
        d *= 2
    return x


def _hgrn_gates(qp, fp, lb):
    logf = jnp.log(lb + (1.0 - lb) * jax.nn.sigmoid(fp))
    kk = (1.0 - lb) * jax.nn.sigmoid(-fp)
    q = _silu(qp) * (HG_DK ** -0.5)
    return q, kk, logf


def _prompt_mixer_kernel(z_ref, cw_ref, lbl_ref, gn_ref, ab_ref, conv_ref, s_ref, ubuf, st_scr,
                         *, layer_j, tt):
    t = pl.program_id(1)
    nt = pl.num_programs(1)

    @pl.when(t == 0)
    def _():
        ubuf[0:8, :] = jnp.zeros((8, CONV_CH), F32)
        st_scr[...] = jnp.zeros_like(st_scr)

    u = z_ref[:, Z_C:Z_C + CONV_CH] * z_ref[:, Z_X:Z_X + CONV_CH]
    ubuf[8:8 + tt, :] = u
    y = (cw_ref[0:1, :] * ubuf[6:6 + tt, :] + cw_ref[1:2, :] * ubuf[7:7 + tt, :]
         + cw_ref[2:3, :] * ubuf[8:8 + tt, :])
    ab_ref[:, 0:CONV_CH] = (z_ref[:, Z_B:Z_B + CONV_CH] * y).astype(BF16)
    last2 = ubuf[tt + 6:tt + 8, :]
    ubuf[6:8, :] = last2
    conv_ref[...] = last2

    lb_all = _hg_lower_bound(lbl_ref, layer_j)
    gn = gn_ref[...]
    ell = HG_SUB
    row = lax.broadcasted_iota(jnp.int32, (ell, HG_DK), 0)

    def chunk(c, carry):
        r0 = pl.multiple_of(c * ell, ell)
        for h in range(HG_HEADS):
            lo = h * HG_DK
            lb = lb_all[:, lo:lo + HG_DK]
            qp = z_ref[pl.ds(r0, ell), Z_Q + lo:Z_Q + lo + HG_DK]
            fp = z_ref[pl.ds(r0, ell), Z_F + lo:Z_F + lo + HG_DK]
            v = z_ref[pl.ds(r0, ell), Z_I + lo:Z_I + lo + HG_DK]
            gp = z_ref[pl.ds(r0, ell), Z_G + lo:Z_G + lo + HG_DK]
            q, kk, logf = _hgrn_gates(qp, fp, lb)
            gc = _cumsum_rows(logf)
            gl = gc[ell - 1:ell, :]
            st = st_scr[h]
            o = _dot_nt((q * jnp.exp(gc)).astype(BF16), st.astype(BF16))
            for s in range(ell):
                dec = jnp.exp(jnp.where(row >= s, gc - gc[s:s + 1, :], NEG))
                a = jnp.sum(q * kk[s:s + 1, :] * dec, axis=-1, keepdims=True)
                o = o + a * v[s:s + 1, :]
            kd = kk * jnp.exp(gl - gc)
            st_scr[h] = st * jnp.exp(gl) + _dot_tn(v.astype(BF16), kd.astype(BF16))
            b = _rmsnorm(o, gn) * _silu(gp)
            ab_ref[pl.ds(r0, ell), CONV_CH + lo:CONV_CH + lo + HG_DK] = b.astype(BF16)
        return carry

    lax.fori_loop(0, tt // ell, chunk, 0)

    @pl.when(t == nt - 1)
    def _():
        for h in range(HG_HEADS):
            s_ref[h] = st_scr[h].T


def _prompt_mixer(z, conv_w_j, lb_logits, gnorm_j, layer_j, nb, seq, tt):
    m = z.shape[0]
    nt = seq // tt
    return pl.pallas_call(
        functools.partial(_prompt_mixer_kernel, layer_j=layer_j, tt=tt),
        grid=(nb, nt),
        in_specs=[
            pl.BlockSpec((tt, EVEN_IN), lambda n, t: (n * nt + t, 0)),
            pl.BlockSpec((CONV_W, CONV_CH), lambda n, t: (0, 0)),
            pl.BlockSpec(lb_logits.shape, lambda n, t: (0, 0)),
            pl.BlockSpec((1, HG_DK), lambda n, t: (0, 0)),
        ],
        out_specs=[
            pl.BlockSpec((tt, D_MODEL), lambda n, t: (n * nt + t, 0)),
            pl.BlockSpec((None, CONV_W - 1, CONV_CH), lambda n, t: (n, 0, 0)),
            pl.BlockSpec((None, HG_HEADS, HG_DK, HG_DK), lambda n, t: (n, 0, 0, 0)),
        ],
        out_shape=[
            jax.ShapeDtypeStruct((m, D_MODEL), BF16),
            jax.ShapeDtypeStruct((nb, CONV_W - 1, CONV_CH), F32),
            jax.ShapeDtypeStruct((nb, HG_HEADS, HG_DK, HG_DK), F32),
        ],
        scratch_shapes=[pltpu.VMEM((tt + 8, CONV_CH), F32),
                        pltpu.VMEM((HG_HEADS, HG_DK, HG_DK), F32)],
        compiler_params=_cp(("parallel", "arbitrary")),
        name="prompt_conv_hgrn",
    )(z, conv_w_j, lb_logits, gnorm_j)


def _column(row_vec, eye):
    return jnp.sum(jnp.where(eye, row_vec, 0.0), axis=1, keepdims=True)


def _sample_mixer_kernel(z_ref, cb_ref, s0_ref, cw_ref, lbl_ref, gn_ref, ab_ref, conv_ref, s_ref,
                         b_scr, *, layer_j, tb):
    u = z_ref[:, Z_C:Z_C + CONV_CH] * z_ref[:, Z_X:Z_X + CONV_CH]
    b0 = cb_ref[:, 0, :]
    b1 = cb_ref[:, 1, :]
    y = cw_ref[0:1, :] * b0 + cw_ref[1:2, :] * b1 + cw_ref[2:3, :] * u
    ab_ref[:, 0:CONV_CH] = (z_ref[:, Z_B:Z_B + CONV_CH] * y).astype(BF16)
    conv_ref[:, 0, :] = b1
    conv_ref[:, 1, :] = u

    lb_all = _hg_lower_bound(lbl_ref, layer_j)
    gn = gn_ref[...]
    eye = (lax.broadcasted_iota(jnp.int32, (HG_DK, HG_DK), 0)
           == lax.broadcasted_iota(jnp.int32, (HG_DK, HG_DK), 1))

    q_all, kk_all, logf_all = _hgrn_gates(z_ref[:, Z_Q:Z_Q + HG_WIDTH], z_ref[:, Z_F:Z_F + HG_WIDTH],
                                          lb_all)
    ef_all = jnp.exp(logf_all)
    v_all = z_ref[:, Z_I:Z_I + HG_WIDTH]
    for n in range(tb):
        for h in range(HG_HEADS):
            lo = h * HG_DK
            row = lambda a: a[n:n + 1, lo:lo + HG_DK]
            s_new = (_column(row(ef_all), eye) * s0_ref[n, h]
                     + _column(row(kk_all), eye) * row(v_all))
            s_ref[n, h] = s_new
            b_scr[n:n + 1, lo:lo + HG_DK] = jnp.sum(_column(row(q_all), eye) * s_new, axis=0,
                                                    keepdims=True)
    for h in range(HG_HEADS):
        lo = h * HG_DK
        b = _rmsnorm(b_scr[:, lo:lo + HG_DK], gn) * _silu(z_ref[:, Z_G + lo:Z_G + lo + HG_DK])
        ab_ref[:, CONV_CH + lo:CONV_CH + lo + HG_DK] = b.astype(BF16)


def _sample_mixer(z, conv_buf, s0, conv_w_j, lb_logits, gnorm_j, layer_j, tb):
    nb = z.shape[0]
    return pl.pallas_call(
        functools.partial(_sample_mixer_kernel, layer_j=layer_j, tb=tb),
        grid=(nb // tb,),
        in_specs=[
            pl.BlockSpec((tb, EVEN_IN), lambda i: (i, 0)),
            pl.BlockSpec((tb, CONV_W - 1, CONV_CH), lambda i: (i, 0, 0)),
            pl.BlockSpec((tb, HG_HEADS, HG_DK, HG_DK), lambda i: (i, 0, 0, 0)),
            pl.BlockSpec((CONV_W, CONV_CH), lambda i: (0, 0)),
            pl.BlockSpec(lb_logits.shape, lambda i: (0, 0)),
            pl.BlockSpec((1, HG_DK), lambda i: (0, 0)),
        ],
        out_specs=[
            pl.BlockSpec((tb, D_MODEL), lambda i: (i, 0)),
            pl.BlockSpec((tb, CONV_W - 1, CONV_CH), lambda i: (i, 0, 0)),
            pl.BlockSpec((tb, HG_HEADS, HG_DK, HG_DK), lambda i: (i, 0, 0, 0)),
        ],
        out_shape=[
            jax.ShapeDtypeStruct((nb, D_MODEL), BF16),
            jax.ShapeDtypeStruct((nb, CONV_W - 1, CONV_CH), F32),
            jax.ShapeDtypeStruct((nb, HG_HEADS, HG_DK, HG_DK), F32),
        ],
        scratch_shapes=[pltpu.VMEM((tb, HG_WIDTH), F32)],
        compiler_params=_cp(("parallel",)),
        name="sample_conv_hgrn",
    )(z, conv_buf, s0, conv_w_j, lb_logits, gnorm_j)


def _rope_table_kernel(cos_ref, sin_ref, *, tr, pos0, step):
    lane = lax.broadcasted_iota(jnp.int32, (tr, LANES), 1)
    rowi = lax.broadcasted_iota(jnp.int32, (tr, LANES), 0)
    half = ROPE_D // 2
    fi = (lane % half).astype(F32)
    inv = jnp.exp(fi * (-math.log(ROPE_THETA) / half))
    pos = (pos0 + step * (pl.program_id(0) * tr + rowi)).astype(F32)
    ang = pos * inv
    sign = jnp.where((lane % ROPE_D) < half, -1.0, 1.0)
    cos_ref[...] = jnp.cos(ang)
    sin_ref[...] = jnp.sin(ang) * sign


def _rope_table(rows, pos0, step, tr):
    return pl.pallas_call(
        functools.partial(_rope_table_kernel, tr=tr, pos0=pos0, step=step),
        grid=(rows // tr,),
        out_specs=[pl.BlockSpec((tr, LANES), lambda i: (i, 0))] * 2,
        out_shape=[jax.ShapeDtypeStruct((rows, LANES), F32)] * 2,
        compiler_params=_cp(("parallel",)),
        name="rope_table",
    )()


def _rope_pairs(g, cos, sin_signed):
    lane = lax.broadcasted_iota(jnp.int32, g.shape, 1)
    half = ROPE_D // 2
    n = g.shape[1]
    rot = jnp.where((lane % ROPE_D) < half, pltpu.roll(g, n - half, 1), pltpu.roll(g, half, 1))
    return g * cos + rot * sin_signed


def _dqkv_kernel(x_ref, g_ref, sh_ref, sc_ref, w_ref, qn_ref, kvn_ref, cos_ref, sin_ref,
                 cq_ref, ckv_ref, kr_ref, ckvb_ref, krb_ref):
    h = _rmsnorm(x_ref[...], g_ref[...]) * (1.0 + sc_ref[...]) + sh_ref[...]
    d = _dot(h.astype(BF16), w_ref[...])
    cq_ref[...] = _rmsnorm(d[:, :Q_LORA], qn_ref[...]).astype(BF16)
    ckv = _rmsnorm(d[:, Q_LORA:Q_LORA + KV_LORA], kvn_ref[...])
    ckv_ref[...] = ckv
    ckvb_ref[...] = ckv.astype(BF16)
    kr = _rope_pairs(d[:, Q_LORA + KV_LORA:], cos_ref[...], sin_ref[...])[:, :ROPE_D]
    kr_ref[...] = kr
    krb_ref[...] = kr.astype(BF16)


def _dqkv(x, g, mod, layer, w_pad, q_norm, kv_norm, cos_t, sin_t, tm, rope_blocks):
    m, d = x.shape
    n = w_pad.shape[1]
    rb = rope_blocks
    return pl.pallas_call(
        _dqkv_kernel,
        grid=(m // tm,),
        in_specs=[
            pl.BlockSpec((tm, d), lambda i: (i, 0)),
            pl.BlockSpec((1, d), lambda i: (0, 0)),
            mod.spec(layer, 0),
            mod.spec(layer, 1),
            pl.BlockSpec((d, n), lambda i: (0, 0)),
            pl.BlockSpec((1, Q_LORA), lambda i: (0, 0)),
            pl.BlockSpec((1, KV_LORA), lambda i: (0, 0)),
            pl.BlockSpec((cos_t.shape[0] // rb, LANES), lambda i: (i % rb, 0)),
            pl.BlockSpec((cos_t.shape[0] // rb, LANES), lambda i: (i % rb, 0)),
        ],
        out_specs=[
            pl.BlockSpec((tm, Q_LORA), lambda i: (i, 0)),
            pl.BlockSpec((tm, KV_LORA), lambda i: (i, 0)),
            pl.BlockSpec((tm, ROPE_D), lambda i: (i, 0)),
            pl.BlockSpec((tm, KV_LORA), lambda i: (i, 0)),
            pl.BlockSpec((tm, ROPE_D), lambda i: (i, 0)),
        ],
        out_shape=[
            jax.ShapeDtypeStruct((m, Q_LORA), BF16),
            jax.ShapeDtypeStruct((m, KV_LORA), F32),
            jax.ShapeDtypeStruct((m, ROPE_D), F32),
            jax.ShapeDtypeStruct((m, KV_LORA), BF16),
            jax.ShapeDtypeStruct((m, ROPE_D), BF16),
        ],
        compiler_params=_cp(("parallel",)),
        name="mla_down_proj",
    )(x, g, mod.arr, mod.arr, w_pad, q_norm, kv_norm, cos_t, sin_t)


def _q_kernel(cq_ref, wn_ref, wr_ref, wuk_ref, cos_ref, sin_ref, ql_ref, qr_ref):
    cq = cq_ref[...]
    qn = _dot(cq, wn_ref[...])
    qr = _dot(cq, wr_ref[...])
    cos = jnp.concatenate([cos_ref[...]] * (MLA_HEADS * ROPE_D // LANES), axis=1)
    sin = jnp.concatenate([sin_ref[...]] * (MLA_HEADS * ROPE_D // LANES), axis=1)
    qr = _rope_pairs(qr, cos, sin).astype(BF16)
    for h in range(MLA_HEADS):
        ql_ref[h] = _dot(qn[:, h * NOPE:(h + 1) * NOPE].astype(BF16), wuk_ref[h]).astype(BF16)
        qr_ref[h] = qr[:, h * ROPE_D:(h + 1) * ROPE_D]


def _q_proj(cq, w_nope, w_rope, w_ukt, cos_t, sin_t, nb, seq, tm, rope_blocks):
    nt = seq // tm
    rb = rope_blocks
    return pl.pallas_call(
        _q_kernel,
        grid=(nb, nt),
        in_specs=[
            pl.BlockSpec((tm, Q_LORA), lambda n, t: (n * nt + t, 0)),
            pl.BlockSpec(w_nope.shape, lambda n, t: (0, 0)),
            pl.BlockSpec(w_rope.shape, lambda n, t: (0, 0)),
            pl.BlockSpec(w_ukt.shape, lambda n, t: (0, 0, 0)),
            pl.BlockSpec((cos_t.shape[0] // rb, LANES), lambda n, t: (t % rb, 0)),
            pl.BlockSpec((cos_t.shape[0] // rb, LANES), lambda n, t: (t % rb, 0)),
        ],
        out_specs=[
            pl.BlockSpec((None, MLA_HEADS, tm, KV_LORA), lambda n, t: (n, 0, t, 0)),
            pl.BlockSpec((None, MLA_HEADS, tm, ROPE_D), lambda n, t: (n, 0, t, 0)),
        ],
        out_shape=[
            jax.ShapeDtypeStruct((nb, MLA_HEADS, seq, KV_LORA), BF16),
            jax.ShapeDtypeStruct((nb, MLA_HEADS, seq, ROPE_D), BF16),
        ],
        compiler_params=_cp(("parallel", "parallel")),
        name="mla_q_proj",
    )(cq, w_nope, w_rope, w_ukt, cos_t, sin_t)


def _prompt_attn_kernel(ql_ref, qr_ref, k_ref, kr_ref, o_ref, m_scr, l_scr, acc_scr, *, tq):
    i = pl.program_id(1)
    rows = MLA_HEADS * tq
    ql = ql_ref[...].reshape(rows, KV_LORA)
    qr = qr_ref[...].reshape(rows, ROPE_D)
    m_scr[...] = jnp.full_like(m_scr, NEG)
    l_scr[...] = jnp.zeros_like(l_scr)
    acc_scr[...] = jnp.zeros_like(acc_scr)

    def block(j, masked):
        r0 = pl.multiple_of(j * tq, tq)
        k = k_ref[pl.ds(r0, tq), :]
        kr = kr_ref[pl.ds(r0, tq), :]
        s = (_dot_nt(ql, k) + _dot_nt(qr, kr)) * MLA_SCALE
        if masked:
            qpos = lax.broadcasted_iota(jnp.int32, (MLA_HEADS, tq, tq), 1).reshape(rows, tq)
            kpos = lax.broadcasted_iota(jnp.int32, (rows, tq), 1)
            s = jnp.where(kpos <= qpos, s, NEG)
        m_prev = m_scr[...]
        m_new = jnp.maximum(m_prev, jnp.max(s, axis=-1, keepdims=True))
        alpha = jnp.exp(m_prev - m_new)
        p = jnp.exp(s - m_new)
        l_scr[...] = alpha * l_scr[...] + jnp.sum(p, axis=-1, keepdims=True)
        acc_scr[...] = alpha * acc_scr[...] + _dot(p.astype(BF16), k)
        m_scr[...] = m_new

    def body(j, carry):
        block(j, False)
        return carry

    lax.fori_loop(0, i, body, 0)
    block(i, True)
    ctx = acc_scr[...] / l_scr[...]
    for h in range(MLA_HEADS):
        o_ref[:, h * KV_LORA:(h + 1) * KV_LORA] = ctx[h * tq:(h + 1) * tq, :].astype(BF16)


def _prompt_attn(q_lat, q_rope, ckv_b, kr_b, nb, seq, tq):
    nq = seq // tq
    rows = MLA_HEADS * tq
    return pl.pallas_call(
        functools.partial(_prompt_attn_kernel, tq=tq),
        grid=(nb, nq),
        in_specs=[
            pl.BlockSpec((None, MLA_HEADS, tq, KV_LORA), lambda n, i: (n, 0, i, 0)),
            pl.BlockSpec((None, MLA_HEADS, tq, ROPE_D), lambda n, i: (n, 0, i, 0)),
            pl.BlockSpec((seq, KV_LORA), lambda n, i: (n, 0)),
            pl.BlockSpec((seq, ROPE_D), lambda n, i: (n, 0)),
        ],
        out_specs=pl.BlockSpec((tq, MLA_HEADS * KV_LORA), lambda n, i: (n * nq + i, 0)),
        out_shape=jax.ShapeDtypeStruct((nb * seq, MLA_HEADS * KV_LORA), BF16),
        scratch_shapes=[pltpu.VMEM((rows, 1), F32), pltpu.VMEM((rows, 1), F32),
                        pltpu.VMEM((rows, KV_LORA), F32)],
        compiler_params=_cp(("parallel", "arbitrary")),
        name="mla_prompt_attn",
    )(q_lat, q_rope, ckv_b, kr_b)


def _sample_attn_kernel(pt_ref, ql_ref, qr_ref, cn_ref, krn_ref, *rest, pg):
    del pt_ref
    ckv_refs = rest[:pg]
    kr_refs = rest[pg:2 * pg]
    o_ref, m_scr, l_scr, acc_scr = rest[2 * pg:]
    g = pl.program_id(1)

    @pl.when(g == 0)
    def _():
        m_scr[...] = jnp.full_like(m_scr, NEG)
        l_scr[...] = jnp.zeros_like(l_scr)
        acc_scr[...] = jnp.zeros_like(acc_scr)

    ql = ql_ref[...]
    qr = qr_ref[...]
    pages = [r[...].astype(BF16) for r in ckv_refs]
    scores = [(_dot_nt(ql, c) + _dot_nt(qr, r[...].astype(BF16))) * MLA_SCALE
              for c, r in zip(pages, kr_refs)]
    m_prev = m_scr[...]
    m_new = m_prev
    for s in scores:
        m_new = jnp.maximum(m_new, jnp.max(s, axis=-1, keepdims=True))
    alpha = jnp.exp(m_prev - m_new)
    l_new = alpha * l_scr[...]
    acc = alpha * acc_scr[...]
    for s, c in zip(scores, pages):
        p = jnp.exp(s - m_new)
        l_new = l_new + jnp.sum(p, axis=-1, keepdims=True)
        acc = acc + _dot(p.astype(BF16), c)
    m_scr[...] = m_new
    l_scr[...] = l_new
    acc_scr[...] = acc

    @pl.when(g == pl.num_programs(1) - 1)
    def _():
        cn = cn_ref[...]
        krn = krn_ref[...]
        s_new = (jnp.sum(ql.astype(F32) * cn, axis=-1, keepdims=True)
                 + jnp.sum(qr.astype(F32) * krn, axis=-1, keepdims=True)) * MLA_SCALE
        m_fin = jnp.maximum(m_new, s_new)
        a = jnp.exp(m_new - m_fin)
        p_new = jnp.exp(s_new - m_fin)
        l_fin = a * l_new + p_new
        ctx = (a * acc + p_new * cn) / l_fin
        o_ref[...] = ctx.astype(BF16)


def _sample_attn(page_table, q_lat, q_rope, ckv_new, kr_new, cache_ckv, cache_krope, layer_j, pg):
    nb, n_pages = page_table.shape
    ng = n_pages // pg

    def page_spec(width, k):
        return pl.BlockSpec((None, None, PAGE_SIZE, width),
                            lambda b, g, pt: (layer_j, pt[b, g * pg + k], 0, 0))

    grid_spec = pltpu.PrefetchScalarGridSpec(
        num_scalar_prefetch=1,
        grid=(nb, ng),
        in_specs=[
            pl.BlockSpec((None, MLA_HEADS, KV_LORA), lambda b, g, pt: (b, 0, 0)),
            pl.BlockSpec((None, MLA_HEADS, ROPE_D), lambda b, g, pt: (b, 0, 0)),
            pl.BlockSpec((None, 1, KV_LORA), lambda b, g, pt: (b, 0, 0)),
            pl.BlockSpec((None, 1, ROPE_D), lambda b, g, pt: (b, 0, 0)),
        ] + [page_spec(KV_LORA, k) for k in range(pg)] + [page_spec(ROPE_D, k) for k in range(pg)],
        out_specs=pl.BlockSpec((None, MLA_HEADS, KV_LORA), lambda b, g, pt: (b, 0, 0)),
        scratch_shapes=[pltpu.VMEM((MLA_HEADS, 1), F32), pltpu.VMEM((MLA_HEADS, 1), F32),
                        pltpu.VMEM((MLA_HEADS, KV_LORA), F32)],
    )
    return pl.pallas_call(
        functools.partial(_sample_attn_kernel, pg=pg),
        grid_spec=grid_spec,
        out_shape=jax.ShapeDtypeStruct((nb, MLA_HEADS, KV_LORA), BF16),
        compiler_params=_cp(("parallel", "arbitrary")),
        name="mla_sample_attn",
    )(page_table, q_lat, q_rope, ckv_new.reshape(nb, 1, KV_LORA), kr_new.reshape(nb, 1, ROPE_D),
      *([cache_ckv] * pg), *([cache_krope] * pg))


def _mla_out_kernel(ctx_ref, wuv_ref, wo_ref, x_ref, gate_ref, o_ref):
    parts = [_dot(ctx_ref[:, h * KV_LORA:(h + 1) * KV_LORA], wuv_ref[h]).astype(BF16)
             for h in range(MLA_HEADS)]
    o = jnp.concatenate(parts, axis=1)
    o_ref[...] = x_ref[...] + gate_ref[...] * _dot(o, wo_ref[...])


def _mla_out(ctx, w_uv_h, w_o, x, mod, layer, tm):
    m, d = x.shape
    return pl.pallas_call(
        _mla_out_kernel,
        grid=(m // tm,),
        in_specs=[
            pl.BlockSpec((tm, MLA_HEADS * KV_LORA), lambda i: (i, 0)),
            pl.BlockSpec(w_uv_h.shape, lambda i: (0, 0, 0)),
            pl.BlockSpec(w_o.shape, lambda i: (0, 0)),
            pl.BlockSpec((tm, d), lambda i: (i, 0)),
            mod.spec(layer, 2),
        ],
        out_specs=pl.BlockSpec((tm, d), lambda i: (i, 0)),
        out_shape=jax.ShapeDtypeStruct((m, d), F32),
        compiler_params=_cp(("parallel",)),
        name="mla_out_proj",
    )(ctx, w_uv_h, w_o, x, mod.arr)


def _tile(m, pref):
    t = min(m, pref)
    assert m % t == 0, (m, t)
    return t


def kernel(x_prompt, x_sample, c_prompt, c_sample, cache_conv, state_hgrn, cache_ckv, cache_krope,
           page_table, w_ada, b_ada, norm_mix, norm_ffn, w_in_even, conv_w, hg_lb_logits, hg_gnorm,
           w_out_even, w_dqkv, q_norm, w_uq, kv_norm, w_uk, w_uv, w_o, w_gu, w_down, norm_final):
    bp, tp, d = x_prompt.shape
    bs, ts, _ = x_sample.shape
    assert ts == 1 and d == D_MODEL
    depth = w_ada.shape[0]
    n_pages = page_table.shape[1]
    past_len = n_pages * PAGE_SIZE
    mp, ms = bp * tp, bs

    tm_p = _tile(tp, 512)
    tm_s = ms
    tt = _tile(tp, 256)
    tq = _tile(tp, 256)
    tb = _tile(bs, 16)
    pg = _tile(n_pages, 16)
    tf = 1408 if w_down.shape[1] % 1408 == 0 else LANES

    mods = _ada_all(jnp.concatenate([c_prompt, c_sample], axis=0), w_ada, b_ada)
    mod_p = _Mod(mods[:, :bp].reshape(depth, bp, 1, N_MOD * d), tp // tm_p)
    mod_s = _Mod(mods[:, bp:].reshape(depth, 1, bs, N_MOD * d), 1)

    cos_p, sin_p = _rope_table(tp, 0, 1, _tile(tp, 512))
    cos_s, sin_s = _rope_table(8, past_len, 0, 8)
    cos_s = jnp.broadcast_to(cos_s[:1], (ms, LANES))
    sin_s = jnp.broadcast_to(sin_s[:1], (ms, LANES))

    xp = x_prompt.reshape(mp, d)
    xs = x_sample.reshape(ms, d)
    outs = {k: [] for k in ("conv_p", "hg_p", "ckv_p", "kr_p", "conv_s", "hg_s", "ckv_s", "kr_s")}

    for l in range(depth):
        j = l // 2
        g_mix = norm_mix[l].reshape(1, d)
        g_ffn = norm_ffn[l].reshape(1, d)
        if l % 2 == 0:
            w_in = w_in_even[j].astype(BF16)
            w_out = w_out_even[j].astype(BF16)
            gn = hg_gnorm[j].reshape(1, HG_DK)
            zp = _mod_matmul(xp, g_mix, mod_p, l, w_in, tm_p, 512)
            zs = _mod_matmul(xs, g_mix, mod_s, l, w_in, tm_s, 512)
            abp, cvp, hgp = _prompt_mixer(zp, conv_w[j], hg_lb_logits, gn, j, bp, tp, tt)
            abs_, cvs, hgs = _sample_mixer(zs, cache_conv[j], state_hgrn[j], conv_w[j], hg_lb_logits,
                                           gn, j, tb)
            xp = _res_matmul(abp, w_out, xp, mod_p, l, 2, tm_p, 512)
            xs = _res_matmul(abs_, w_out, xs, mod_s, l, 2, tm_s, 512)
            outs["conv_p"].append(cvp)
            outs["hg_p"].append(hgp)
            outs["conv_s"].append(cvs)
            outs["hg_s"].append(hgs)
        else:
            w_d = jnp.pad(w_dqkv[j], ((0, 0), (0, ROPE_D))).astype(BF16)
            wq = w_uq[j].reshape(Q_LORA, MLA_HEADS, NOPE + ROPE_D)
            w_nope = wq[:, :, :NOPE].reshape(Q_LORA, MLA_HEADS * NOPE).astype(BF16)
            w_rope = wq[:, :, NOPE:].reshape(Q_LORA, MLA_HEADS * ROPE_D).astype(BF16)
            w_ukt = jnp.transpose(w_uk[j], (1, 2, 0)).astype(BF16)
            w_uvh = jnp.transpose(w_uv[j], (1, 0, 2)).astype(BF16)
            w_oj = w_o[j].astype(BF16)
            qn = q_norm[j].reshape(1, Q_LORA)
            kvn = kv_norm[j].reshape(1, KV_LORA)

            cqp, ckvp, krp, ckvpb, krpb = _dqkv(xp, g_mix, mod_p, l, w_d, qn, kvn, cos_p, sin_p,
                                                tm_p, tp // tm_p)
            cqs, ckvs, krs, _, _ = _dqkv(xs, g_mix, mod_s, l, w_d, qn, kvn, cos_s, sin_s, tm_s, 1)
            qlp, qrp = _q_proj(cqp, w_nope, w_rope, w_ukt, cos_p, sin_p, bp, tp, tm_p, tp // tm_p)
            qls, qrs = _q_proj(cqs, w_nope, w_rope, w_ukt, cos_s, sin_s, 1, ms, tm_s, 1)
            ctxp = _prompt_attn(qlp, qrp, ckvpb, krpb, bp, tp, tq)
            ctxs = _sample_attn(page_table,
                                jnp.transpose(qls[0], (1, 0, 2)), jnp.transpose(qrs[0], (1, 0, 2)),
                                ckvs, krs, cache_ckv, cache_krope, j, pg)
            xp = _mla_out(ctxp, w_uvh, w_oj, xp, mod_p, l, tm_p)
            xs = _mla_out(ctxs.reshape(ms, MLA_HEADS * KV_LORA), w_uvh, w_oj, xs, mod_s, l, tm_s)
            outs["ckv_p"].append(ckvp.reshape(bp, tp, KV_LORA))
            outs["kr_p"].append(krp.reshape(bp, tp, ROPE_D))
            outs["ckv_s"].append(ckvs.reshape(bs, ts, KV_LORA))
            outs["kr_s"].append(krs.reshape(bs, ts, ROPE_D))

        w_gu_l = w_gu[l].astype(BF16)
        w_dn_l = w_down[l].astype(BF16)
        last = l == depth - 1
        gfin = norm_final.reshape(1, d)
        xp = _ffn(xp, g_ffn, mod_p, l, w_gu_l, w_dn_l, gfin, last, tm_p, tf)
        xs = _ffn(xs, g_ffn, mod_s, l, w_gu_l, w_dn_l, gfin, last, tm_s, tf)

    return (xp.reshape(bp, tp, d), xs.reshape(bs, ts, d),
            jnp.stack(outs["conv_p"]), jnp.stack(outs["hg_p"]),
            jnp.stack(outs["ckv_p"]), jnp.stack(outs["kr_p"]),
            jnp.stack(outs["conv_s"]), jnp.stack(outs["hg_s"]),
            jnp.stack(outs["ckv_s"]), jnp.stack(outs["kr_s"]))
```

```python
import functools
import math

import jax
import jax.numpy as jnp
from jax import lax
from jax.experimental import pallas as pl
from jax.experimental.pallas import tpu as pltpu

F32 = jnp.float32
BF16 = jnp.bfloat16

D_MODEL = 1024
N_MOD = 6
EPS = 1e-6
NEG = -1e30
PAGE_SIZE = 128
CONV_CH = D_MODEL // 2
CONV_W = 3
HG_WIDTH = D_MODEL // 2
HG_DK = 128
HG_HEADS = HG_WIDTH // HG_DK
EVEN_IN = 3 * CONV_CH + 4 * HG_WIDTH
MLA_HEADS = 8
NOPE = 128
ROPE_D = 64
V_D = 128
Q_LORA = 384
KV_LORA = 256
ROPE_THETA = 10000.0
MLA_SCALE = (NOPE + ROPE_D) ** -0.5
LANES = 128

Z_B, Z_C, Z_X = 0, CONV_CH, 2 * CONV_CH
Z_Q = 3 * CONV_CH
Z_F = Z_Q + HG_WIDTH
Z_I = Z_F + HG_WIDTH
Z_G = Z_I + HG_WIDTH

HG_SUB = 16
VMEM_LIMIT = 56 * 1024 * 1024


def _cp(sem, vmem=VMEM_LIMIT):
    return pltpu.CompilerParams(dimension_semantics=sem, vmem_limit_bytes=vmem)


def _silu(x):
    return x * jax.nn.sigmoid(x)


def _rmsnorm(x, g):
    return x * lax.rsqrt(jnp.mean(x * x, axis=-1, keepdims=True) + EPS) * g


def _dot(a, b):
    return jnp.dot(a, b, preferred_element_type=F32)


def _dot_nt(a, b):
    return lax.dot_general(a, b, (((1,), (1,)), ((), ())), preferred_element_type=F32)


def _dot_tn(a, b):
    return lax.dot_general(a, b, (((0,), (0,)), ((), ())), preferred_element_type=F32)


def _ada_kernel(c_ref, w_ref, b_ref, o_ref):
    a = _silu(c_ref[...]).astype(BF16)
    o_ref[...] = _dot(a, w_ref[...].astype(BF16)) + b_ref[...]


def _ada_all(c_all, w_ada, b_ada, tn=1536):
    depth, d, n6 = w_ada.shape
    rows = c_all.shape[0]
    return pl.pallas_call(
        _ada_kernel,
        grid=(depth, n6 // tn),
        in_specs=[
            pl.BlockSpec((rows, d), lambda l, j: (0, 0)),
            pl.BlockSpec((None, d, tn), lambda l, j: (l, 0, j)),
            pl.BlockSpec((None, 1, tn), lambda l, j: (l, 0, j)),
        ],
        out_specs=pl.BlockSpec((None, rows, tn), lambda l, j: (l, 0, j)),
        out_shape=jax.ShapeDtypeStruct((depth, rows, n6), F32),
        compiler_params=_cp(("parallel", "parallel")),
        name="adaln_mod",
    )(c_all, w_ada, b_ada.reshape(depth, 1, n6))


class _Mod:
    def __init__(self, arr, tps):
        self.arr = arr
        self.tps = tps
        self.r = arr.shape[2]

    def spec(self, layer, col, width=D_MODEL, ncol=None):
        tps = self.tps
        per = D_MODEL // width
        if ncol is None:
            return pl.BlockSpec((None, None, self.r, width),
                                lambda i, *_: (layer, i // tps, 0, col * per))
        return pl.BlockSpec((None, None, self.r, width),
                            lambda i, j, *_: (layer, i // tps, 0, col * per + j))


def _modmm_kernel(x_ref, g_ref, sh_ref, sc_ref, w_ref, o_ref, h_scr):
    @pl.when(pl.program_id(1) == 0)
    def _():
        h = _rmsnorm(x_ref[...], g_ref[...]) * (1.0 + sc_ref[...]) + sh_ref[...]
        h_scr[...] = h.astype(BF16)

    o_ref[...] = _dot(h_scr[...], w_ref[...]).astype(o_ref.dtype)


def _mod_matmul(x, g, mod, layer, w, tm, tn, out_dtype=F32):
    m, d = x.shape
    n = w.shape[1]
    return pl.pallas_call(
        _modmm_kernel,
        grid=(m // tm, n // tn),
        in_specs=[
            pl.BlockSpec((tm, d), lambda i, j: (i, 0)),
            pl.BlockSpec((1, d), lambda i, j: (0, 0)),
            mod.spec(layer, 0),
            mod.spec(layer, 1),
            pl.BlockSpec((d, tn), lambda i, j: (0, j)),
        ],
        out_specs=pl.BlockSpec((tm, tn), lambda i, j: (i, j)),
        out_shape=jax.ShapeDtypeStruct((m, n), out_dtype),
        scratch_shapes=[pltpu.VMEM((tm, d), BF16)],
        compiler_params=_cp(("parallel", "arbitrary")),
        name="mod_matmul",
    )(x, g, mod.arr, mod.arr, w)


def _resmm_kernel(a_ref, w_ref, x_ref, gate_ref, o_ref):
    o_ref[...] = x_ref[...] + gate_ref[...] * _dot(a_ref[...], w_ref[...])


def _res_matmul(a, w, x, mod, layer, gate_col, tm, tn):
    m, k = a.shape
    n = w.shape[1]
    return pl.pallas_call(
        _resmm_kernel,
        grid=(m // tm, n // tn),
        in_specs=[
            pl.BlockSpec((tm, k), lambda i, j: (i, 0)),
            pl.BlockSpec((k, tn), lambda i, j: (0, j)),
            pl.BlockSpec((tm, tn), lambda i, j: (i, j)),
            mod.spec(layer, gate_col, width=tn, ncol=True),
        ],
        out_specs=pl.BlockSpec((tm, tn), lambda i, j: (i, j)),
        out_shape=jax.ShapeDtypeStruct((m, n), F32),
        compiler_params=_cp(("parallel", "parallel")),
        name="res_matmul",
    )(a, w, x, mod.arr)


def _ffn_kernel(x_ref, g_ref, sh_ref, sc_ref, gate_ref, wg_ref, wu_ref, wd_ref, gf_ref, o_ref,
                h_scr, acc_scr, *, final_norm):
    f = pl.program_id(1)

    @pl.when(f == 0)
    def _():
        h = _rmsnorm(x_ref[...], g_ref[...]) * (1.0 + sc_ref[...]) + sh_ref[...]
        h_scr[...] = h.astype(BF16)
        acc_scr[...] = jnp.zeros_like(acc_scr)

    h = h_scr[...]
    gg = _dot(h, wg_ref[...])
    uu = _dot(h, wu_ref[...])
    a = (_silu(gg) * uu).astype(BF16)
    acc_scr[...] += _dot(a, wd_ref[...])

    @pl.when(f == pl.num_programs(1) - 1)
    def _():
        y = x_ref[...] + gate_ref[...] * acc_scr[...]
        if final_norm:
            y = _rmsnorm(y, gf_ref[...])
        o_ref[...] = y


def _ffn(x, g, mod, layer, w_gu, w_down, g_final, final_norm, tm, tf):
    m, d = x.shape
    dff = w_down.shape[0]
    nf = dff // tf
    return pl.pallas_call(
        functools.partial(_ffn_kernel, final_norm=final_norm),
        grid=(m // tm, nf),
        in_specs=[
            pl.BlockSpec((tm, d), lambda i, f: (i, 0)),
            pl.BlockSpec((1, d), lambda i, f: (0, 0)),
            mod.spec(layer, 3),
            mod.spec(layer, 4),
            mod.spec(layer, 5),
            pl.BlockSpec((d, tf), lambda i, f: (0, f)),
            pl.BlockSpec((d, tf), lambda i, f: (0, nf + f)),
            pl.BlockSpec((tf, d), lambda i, f: (f, 0)),
            pl.BlockSpec((1, d), lambda i, f: (0, 0)),
        ],
        out_specs=pl.BlockSpec((tm, d), lambda i, f: (i, 0)),
        out_shape=jax.ShapeDtypeStruct((m, d), F32),
        scratch_shapes=[pltpu.VMEM((tm, d), BF16), pltpu.VMEM((tm, d), F32)],
        compiler_params=_cp(("parallel", "arbitrary")),
        name="ffn",
    )(x, g, mod.arr, mod.arr, mod.arr, w_gu, w_gu, w_down, g_final)


def _hg_lower_bound(lbl_ref, j):
    logits = lbl_ref[...]
    e = jnp.exp(logits - jnp.max(logits, axis=0, keepdims=True))
    den = jnp.sum(e, axis=0, keepdims=True)
    lb = jnp.zeros_like(den)
    for i in range(j):
        lb = lb + e[i:i + 1, :] / den
    return lb


def _cumsum_rows(x):
    rows = x.shape[0]
    idx = lax.broadcasted_iota(jnp.int32, x.shape, 0)
    d = 1
    while d < rows:
        x = x + jnp.where(idx >= d, pltpu.roll(x, d, 0), 0.0)
        d *= 2
    return x


def _hgrn_gates(qp, fp, lb):
    logf = jnp.log(lb + (1.0 - lb) * jax.nn.sigmoid(fp))
    kk = (1.0 - lb) * jax.nn.sigmoid(-fp)
    q = _silu(qp) * (HG_DK ** -0.5)
    return q, kk, logf


def _prompt_mixer_kernel(z_ref, cw_ref, lbl_ref, gn_ref, ab_ref, conv_ref, s_ref, ubuf, st_scr,
                         *, layer_j, tt):
    t = pl.program_id(1)
    nt = pl.num_programs(1)

    @pl.when(t == 0)
    def _():
        ubuf[0:8, :] = jnp.zeros((8, CONV_CH), F32)
        st_scr[...] = jnp.zeros_like(st_scr)

    u = z_ref[:, Z_C:Z_C + CONV_CH] * z_ref[:, Z_X:Z_X + CONV_CH]
    ubuf[8:8 + tt, :] = u
    y = (cw_ref[0:1, :] * ubuf[6:6 + tt, :] + cw_ref[1:2, :] * ubuf[7:7 + tt, :]
         + cw_ref[2:3, :] * ubuf[8:8 + tt, :])
    ab_ref[:, 0:CONV_CH] = (z_ref[:, Z_B:Z_B + CONV_CH] * y).astype(BF16)
    last2 = ubuf[tt + 6:tt + 8, :]
    ubuf[6:8, :] = last2
    conv_ref[...] = last2

    lb_all = _hg_lower_bound(lbl_ref, layer_j)
    gn = gn_ref[...]
    ell = HG_SUB
    row = lax.broadcasted_iota(jnp.int32, (ell, HG_DK), 0)

    def chunk(c, carry):
        r0 = pl.multiple_of(c * ell, ell)
        for h in range(HG_HEADS):
            lo = h * HG_DK
            lb = lb_all[:, lo:lo + HG_DK]
            qp = z_ref[pl.ds(r0, ell), Z_Q + lo:Z_Q + lo + HG_DK]
            fp = z_ref[pl.ds(r0, ell), Z_F + lo:Z_F + lo + HG_DK]
            v = z_ref[pl.ds(r0, ell), Z_I + lo:Z_I + lo + HG_DK]
            gp = z_ref[pl.ds(r0, ell), Z_G + lo:Z_G + lo + HG_DK]
            q, kk, logf = _hgrn_gates(qp, fp, lb)
            gc = _cumsum_rows(logf)
            gl = gc[ell - 1:ell, :]
            st = st_scr[h]
            o = _dot_nt((q * jnp.exp(gc)).astype(BF16), st.astype(BF16))
            for s in range(ell):
                dec = jnp.exp(jnp.where(row >= s, gc - gc[s:s + 1, :], NEG))
                a = jnp.sum(q * kk[s:s + 1, :] * dec, axis=-1, keepdims=True)
                o = o + a * v[s:s + 1, :]
            kd = kk * jnp.exp(gl - gc)
            st_scr[h] = st * jnp.exp(gl) + _dot_tn(v.astype(BF16), kd.astype(BF16))
            b = _rmsnorm(o, gn) * _silu(gp)
            ab_ref[pl.ds(r0, ell), CONV_CH + lo:CONV_CH + lo + HG_DK] = b.astype(BF16)
        return carry

    lax.fori_loop(0, tt // ell, chunk, 0)

    @pl.when(t == nt - 1)
    def _():
        for h in range(HG_HEADS):
            s_ref[h] = st_scr[h].T


def _prompt_mixer(z, conv_w_j, lb_logits, gnorm_j, layer_j, nb, seq, tt):
    m = z.shape[0]
    nt = seq // tt
    return pl.pallas_call(
        functools.partial(_prompt_mixer_kernel, layer_j=layer_j, tt=tt),
        grid=(nb, nt),
        in_specs=[
            pl.BlockSpec((tt, EVEN_IN), lambda n, t: (n * nt + t, 0)),
            pl.BlockSpec((CONV_W, CONV_CH), lambda n, t: (0, 0)),
            pl.BlockSpec(lb_logits.shape, lambda n, t: (0, 0)),
            pl.BlockSpec((1, HG_DK), lambda n, t: (0, 0)),
        ],
        out_specs=[
            pl.BlockSpec((tt, D_MODEL), lambda n, t: (n * nt + t, 0)),
            pl.BlockSpec((None, CONV_W - 1, CONV_CH), lambda n, t: (n, 0, 0)),
            pl.BlockSpec((None, HG_HEADS, HG_DK, HG_DK), lambda n, t: (n, 0, 0, 0)),
        ],
        out_shape=[
            jax.ShapeDtypeStruct((m, D_MODEL), BF16),
            jax.ShapeDtypeStruct((nb, CONV_W - 1, CONV_CH), F32),
            jax.ShapeDtypeStruct((nb, HG_HEADS, HG_DK, HG_DK), F32),
        ],
        scratch_shapes=[pltpu.VMEM((tt + 8, CONV_CH), F32),
                        pltpu.VMEM((HG_HEADS, HG_DK, HG_DK), F32)],
        compiler_params=_cp(("parallel", "arbitrary")),
        name="prompt_conv_hgrn",
    )(z, conv_w_j, lb_logits, gnorm_j)


def _column(row_vec, eye):
    return jnp.sum(jnp.where(eye, row_vec, 0.0), axis=1, keepdims=True)


def _sample_mixer_kernel(z_ref, cb_ref, s0_ref, cw_ref, lbl_ref, gn_ref, ab_ref, conv_ref, s_ref,
                         b_scr, *, layer_j, tb):
    u = z_ref[:, Z_C:Z_C + CONV_CH] * z_ref[:, Z_X:Z_X + CONV_CH]
    b0 = cb_ref[:, 0, :]
    b1 = cb_ref[:, 1, :]
    y = cw_ref[0:1, :] * b0 + cw_ref[1:2, :] * b1 + cw_ref[2:3, :] * u
    ab_ref[:, 0:CONV_CH] = (z_ref[:, Z_B:Z_B + CONV_CH] * y).astype(BF16)
    conv_ref[:, 0, :] = b1
    conv_ref[:, 1, :] = u

    lb_all = _hg_lower_bound(lbl_ref, layer_j)
    gn = gn_ref[...]
    eye = (lax.broadcasted_iota(jnp.int32, (HG_DK, HG_DK), 0)
           == lax.broadcasted_iota(jnp.int32, (HG_DK, HG_DK), 1))

    q_all, kk_all, logf_all = _hgrn_gates(z_ref[:, Z_Q:Z_Q + HG_WIDTH], z_ref[:, Z_F:Z_F + HG_WIDTH],
                                          lb_all)
    ef_all = jnp.exp(logf_all)
    v_all = z_ref[:, Z_I:Z_I + HG_WIDTH]
    for n in range(tb):
        for h in range(HG_HEADS):
            lo = h * HG_DK
            row = lambda a: a[n:n + 1, lo:lo + HG_DK]
            s_new = (_column(row(ef_all), eye) * s0_ref[n, h]
                     + _column(row(kk_all), eye) * row(v_all))
            s_ref[n, h] = s_new
            b_scr[n:n + 1, lo:lo + HG_DK] = jnp.sum(_column(row(q_all), eye) * s_new, axis=0,
                                                    keepdims=True)
    for h in range(HG_HEADS):
        lo = h * HG_DK
        b = _rmsnorm(b_scr[:, lo:lo + HG_DK], gn) * _silu(z_ref[:, Z_G + lo:Z_G + lo + HG_DK])
        ab_ref[:, CONV_CH + lo:CONV_CH + lo + HG_DK] = b.astype(BF16)


def _sample_mixer(z, conv_buf, s0, conv_w_j, lb_logits, gnorm_j, layer_j, tb):
    nb = z.shape[0]
    return pl.pallas_call(
        functools.partial(_sample_mixer_kernel, layer_j=layer_j, tb=tb),
        grid=(nb // tb,),
        in_specs=[
            pl.BlockSpec((tb, EVEN_IN), lambda i: (i, 0)),
            pl.BlockSpec((tb, CONV_W - 1, CONV_CH), lambda i: (i, 0, 0)),
            pl.BlockSpec((tb, HG_HEADS, HG_DK, HG_DK), lambda i: (i, 0, 0, 0)),
            pl.BlockSpec((CONV_W, CONV_CH), lambda i: (0, 0)),
            pl.BlockSpec(lb_logits.shape, lambda i: (0, 0)),
            pl.BlockSpec((1, HG_DK), lambda i: (0, 0)),
        ],
        out_specs=[
            pl.BlockSpec((tb, D_MODEL), lambda i: (i, 0)),
            pl.BlockSpec((tb, CONV_W - 1, CONV_CH), lambda i: (i, 0, 0)),
            pl.BlockSpec((tb, HG_HEADS, HG_DK, HG_DK), lambda i: (i, 0, 0, 0)),
        ],
        out_shape=[
            jax.ShapeDtypeStruct((nb, D_MODEL), BF16),
            jax.ShapeDtypeStruct((nb, CONV_W - 1, CONV_CH), F32),
            jax.ShapeDtypeStruct((nb, HG_HEADS, HG_DK, HG_DK), F32),
        ],
        scratch_shapes=[pltpu.VMEM((tb, HG_WIDTH), F32)],
        compiler_params=_cp(("parallel",)),
        name="sample_conv_hgrn",
    )(z, conv_buf, s0, conv_w_j, lb_logits, gnorm_j)


def _rope_table_kernel(cos_ref, sin_ref, *, tr, pos0, step):
    lane = lax.broadcasted_iota(jnp.int32, (tr, LANES), 1)
    rowi = lax.broadcasted_iota(jnp.int32, (tr, LANES), 0)
    half = ROPE_D // 2
    fi = (lane % half).astype(F32)
    inv = jnp.exp(fi * (-math.log(ROPE_THETA) / half))
    pos = (pos0 + step * (pl.program_id(0) * tr + rowi)).astype(F32)
    ang = pos * inv
    sign = jnp.where((lane % ROPE_D) < half, -1.0, 1.0)
    cos_ref[...] = jnp.cos(ang)
    sin_ref[...] = jnp.sin(ang) * sign


def _rope_table(rows, pos0, step, tr):
    return pl.pallas_call(
        functools.partial(_rope_table_kernel, tr=tr, pos0=pos0, step=step),
        grid=(rows // tr,),
        out_specs=[pl.BlockSpec((tr, LANES), lambda i: (i, 0))] * 2,
        out_shape=[jax.ShapeDtypeStruct((rows, LANES), F32)] * 2,
        compiler_params=_cp(("parallel",)),
        name="rope_table",
    )()


def _rope_pairs(g, cos, sin_signed):
    lane = lax.broadcasted_iota(jnp.int32, g.shape, 1)
    half = ROPE_D // 2
    n = g.shape[1]
    rot = jnp.where((lane % ROPE_D) < half, pltpu.roll(g, n - half, 1), pltpu.roll(g, half, 1))
    return g * cos + rot * sin_signed


def _dqkv_kernel(x_ref, g_ref, sh_ref, sc_ref, w_ref, qn_ref, kvn_ref, cos_ref, sin_ref,
                 cq_ref, ckv_ref, kr_ref, ckvb_ref, krb_ref):
    h = _rmsnorm(x_ref[...], g_ref[...]) * (1.0 + sc_ref[...]) + sh_ref[...]
    d = _dot(h.astype(BF16), w_ref[...])
    cq_ref[...] = _rmsnorm(d[:, :Q_LORA], qn_ref[...]).astype(BF16)
    ckv = _rmsnorm(d[:, Q_LORA:Q_LORA + KV_LORA], kvn_ref[...])
    ckv_ref[...] = ckv
    ckvb_ref[...] = ckv.astype(BF16)
    kr = _rope_pairs(d[:, Q_LORA + KV_LORA:], cos_ref[...], sin_ref[...])[:, :ROPE_D]
    kr_ref[...] = kr
    krb_ref[...] = kr.astype(BF16)


def _dqkv(x, g, mod, layer, w_pad, q_norm, kv_norm, cos_t, sin_t, tm, rope_blocks):
    m, d = x.shape
    n = w_pad.shape[1]
    rb = rope_blocks
    return pl.pallas_call(
        _dqkv_kernel,
        grid=(m // tm,),
        in_specs=[
            pl.BlockSpec((tm, d), lambda i: (i, 0)),
            pl.BlockSpec((1, d), lambda i: (0, 0)),
            mod.spec(layer, 0),
            mod.spec(layer, 1),
            pl.BlockSpec((d, n), lambda i: (0, 0)),
            pl.BlockSpec((1, Q_LORA), lambda i: (0, 0)),
            pl.BlockSpec((1, KV_LORA), lambda i: (0, 0)),
            pl.BlockSpec((cos_t.shape[0] // rb, LANES), lambda i: (i % rb, 0)),
            pl.BlockSpec((cos_t.shape[0] // rb, LANES), lambda i: (i % rb, 0)),
        ],
        out_specs=[
            pl.BlockSpec((tm, Q_LORA), lambda i: (i, 0)),
            pl.BlockSpec((tm, KV_LORA), lambda i: (i, 0)),
            pl.BlockSpec((tm, ROPE_D), lambda i: (i, 0)),
            pl.BlockSpec((tm, KV_LORA), lambda i: (i, 0)),
            pl.BlockSpec((tm, ROPE_D), lambda i: (i, 0)),
        ],
        out_shape=[
            jax.ShapeDtypeStruct((m, Q_LORA), BF16),
            jax.ShapeDtypeStruct((m, KV_LORA), F32),
            jax.ShapeDtypeStruct((m, ROPE_D), F32),
            jax.ShapeDtypeStruct((m, KV_LORA), BF16),
            jax.ShapeDtypeStruct((m, ROPE_D), BF16),
        ],
        compiler_params=_cp(("parallel",)),
        name="mla_down_proj",
    )(x, g, mod.arr, mod.arr, w_pad, q_norm, kv_norm, cos_t, sin_t)


def _q_kernel(cq_ref, wn_ref, wr_ref, wuk_ref, cos_ref, sin_ref, ql_ref, qr_ref):
    cq = cq_ref[...]
    qn = _dot(cq, wn_ref[...])
    qr = _dot(cq, wr_ref[...])
    cos = jnp.concatenate([cos_ref[...]] * (MLA_HEADS * ROPE_D // LANES), axis=1)
    sin = jnp.concatenate([sin_ref[...]] * (MLA_HEADS * ROPE_D // LANES), axis=1)
    qr = _rope_pairs(qr, cos, sin).astype(BF16)
    for h in range(MLA_HEADS):
        ql_ref[h] = _dot(qn[:, h * NOPE:(h + 1) * NOPE].astype(BF16), wuk_ref[h]).astype(BF16)
        qr_ref[h] = qr[:, h * ROPE_D:(h + 1) * ROPE_D]


def _q_proj(cq, w_nope, w_rope, w_ukt, cos_t, sin_t, nb, seq, tm, rope_blocks):
    nt = seq // tm
    rb = rope_blocks
    return pl.pallas_call(
        _q_kernel,
        grid=(nb, nt),
        in_specs=[
            pl.BlockSpec((tm, Q_LORA), lambda n, t: (n * nt + t, 0)),
            pl.BlockSpec(w_nope.shape, lambda n, t: (0, 0)),
            pl.BlockSpec(w_rope.shape, lambda n, t: (0, 0)),
            pl.BlockSpec(w_ukt.shape, lambda n, t: (0, 0, 0)),
            pl.BlockSpec((cos_t.shape[0] // rb, LANES), lambda n, t: (t % rb, 0)),
            pl.BlockSpec((cos_t.shape[0] // rb, LANES), lambda n, t: (t % rb, 0)),
        ],
        out_specs=[
            pl.BlockSpec((None, MLA_HEADS, tm, KV_LORA), lambda n, t: (n, 0, t, 0)),
            pl.BlockSpec((None, MLA_HEADS, tm, ROPE_D), lambda n, t: (n, 0, t, 0)),
        ],
        out_shape=[
            jax.ShapeDtypeStruct((nb, MLA_HEADS, seq, KV_LORA), BF16),
            jax.ShapeDtypeStruct((nb, MLA_HEADS, seq, ROPE_D), BF16),
        ],
        compiler_params=_cp(("parallel", "parallel")),
        name="mla_q_proj",
    )(cq, w_nope, w_rope, w_ukt, cos_t, sin_t)


def _lane_tile(x, width):
    return x if width == LANES else jnp.concatenate([x] * (width // LANES), axis=1)


def _prompt_attn_kernel(ql_ref, qr_ref, k_ref, kr_ref, o_ref, m_scr, l_scr, acc_scr, *, tq):
    i = pl.program_id(1)
    m_scr[...] = jnp.full_like(m_scr, NEG)
    l_scr[...] = jnp.zeros_like(l_scr)
    acc_scr[...] = jnp.zeros_like(acc_scr)

    def block(j, masked):
        r0 = pl.multiple_of(j * tq, tq)
        k = k_ref[pl.ds(r0, tq), :]
        kr = kr_ref[pl.ds(r0, tq), :]
        if masked:
            causal = (lax.broadcasted_iota(jnp.int32, (tq, tq), 1)
                      <= lax.broadcasted_iota(jnp.int32, (tq, tq), 0))

        def head(h, carry):
            s = (_dot_nt(ql_ref[h], k) + _dot_nt(qr_ref[h], kr)) * MLA_SCALE
            if masked:
                s = jnp.where(causal, s, NEG)
            m_prev = m_scr[h]
            m_new = jnp.maximum(m_prev, jnp.max(s, axis=-1, keepdims=True))
            alpha = jnp.exp(m_prev - m_new)
            p = jnp.exp(s - _lane_tile(m_new, tq))
            l_scr[h] = alpha * l_scr[h] + jnp.sum(p, axis=-1, keepdims=True)
            acc_scr[h] = _lane_tile(alpha, KV_LORA) * acc_scr[h] + _dot(p.astype(BF16), k)
            m_scr[h] = m_new
            return carry

        lax.fori_loop(0, MLA_HEADS, head, 0, unroll=True)

    def body(j, carry):
        block(j, False)
        return carry

    lax.fori_loop(0, i, body, 0)
    block(i, True)
    for h in range(MLA_HEADS):
        ctx = acc_scr[h] / _lane_tile(l_scr[h], KV_LORA)
        o_ref[:, h * KV_LORA:(h + 1) * KV_LORA] = ctx.astype(BF16)


def _prompt_attn(q_lat, q_rope, ckv_b, kr_b, nb, seq, tq):
    nq = seq // tq
    return pl.pallas_call(
        functools.partial(_prompt_attn_kernel, tq=tq),
        grid=(nb, nq),
        in_specs=[
            pl.BlockSpec((None, MLA_HEADS, tq, KV_LORA), lambda n, i: (n, 0, i, 0)),
            pl.BlockSpec((None, MLA_HEADS, tq, ROPE_D), lambda n, i: (n, 0, i, 0)),
            pl.BlockSpec((seq, KV_LORA), lambda n, i: (n, 0)),
            pl.BlockSpec((seq, ROPE_D), lambda n, i: (n, 0)),
        ],
        out_specs=pl.BlockSpec((tq, MLA_HEADS * KV_LORA), lambda n, i: (n * nq + i, 0)),
        out_shape=jax.ShapeDtypeStruct((nb * seq, MLA_HEADS * KV_LORA), BF16),
        scratch_shapes=[pltpu.VMEM((MLA_HEADS, tq, LANES), F32),
                        pltpu.VMEM((MLA_HEADS, tq, LANES), F32),
                        pltpu.VMEM((MLA_HEADS, tq, KV_LORA), F32)],
        compiler_params=_cp(("parallel", "arbitrary")),
        name="mla_prompt_attn",
    )(q_lat, q_rope, ckv_b, kr_b)


def _sample_attn_kernel(pt_ref, ql_ref, qr_ref, cn_ref, krn_ref, ckv_hbm, krt_hbm, o_ref,
                        kbuf, rbuf, kb16, sems, *, layer_j, n_pages, chunk):
    b = pl.program_id(0)
    nb = pl.num_programs(0)
    slot = b % 2
    past = n_pages * PAGE_SIZE

    def page_copies(seq, slot_, p):
        page = pt_ref[seq, p]
        row0 = pl.multiple_of(p * PAGE_SIZE, PAGE_SIZE)
        return (
            pltpu.make_async_copy(ckv_hbm.at[layer_j, page],
                                  kbuf.at[slot_, pl.ds(row0, PAGE_SIZE), :], sems.at[0, slot_]),
            pltpu.make_async_copy(krt_hbm.at[layer_j, page],
                                  rbuf.at[slot_, :, pl.ds(row0, PAGE_SIZE)], sems.at[1, slot_]),
        )

    def fetch(seq, slot_):
        def start(p, carry):
            for c in page_copies(seq, slot_, p):
                c.start()
            return carry
        lax.fori_loop(0, n_pages, start, 0, unroll=4)

    @pl.when(b == 0)
    def _():
        fetch(0, 0)

    @pl.when(b + 1 < nb)
    def _():
        fetch(b + 1, 1 - slot)

    def wait(p, carry):
        for c in page_copies(b, slot, p):
            c.wait()
        return carry
    lax.fori_loop(0, n_pages, wait, 0, unroll=4)

    ql = ql_ref[...]
    qr = qr_ref[...]
    scores = []
    for c in range(past // chunk):
        kb = kbuf[slot, c * chunk:(c + 1) * chunk, :].astype(BF16)
        kb16[c * chunk:(c + 1) * chunk, :] = kb
        rb = rbuf[slot, :, c * chunk:(c + 1) * chunk].astype(BF16)
        scores.append((_dot_nt(ql, kb) + _dot(qr, rb)) * MLA_SCALE)
    cn = cn_ref[...]
    s_new = (jnp.sum(ql.astype(F32) * cn, axis=-1, keepdims=True)
             + jnp.sum(qr.astype(F32) * krn_ref[...], axis=-1, keepdims=True)) * MLA_SCALE
    m = s_new
    for s in scores:
        m = jnp.maximum(m, jnp.max(s, axis=-1, keepdims=True))
    p_new = jnp.exp(s_new - m)
    l = p_new
    acc = p_new * cn
    for c, s in enumerate(scores):
        p = jnp.exp(s - m)
        l = l + jnp.sum(p, axis=-1, keepdims=True)
        acc = acc + _dot(p.astype(BF16), kb16[c * chunk:(c + 1) * chunk, :])
    o_ref[...] = (acc / l).astype(BF16)


def _sample_attn(page_table, q_lat, q_rope, ckv_new, kr_new, cache_ckv, cache_krope_t, layer_j):
    nb, n_pages = page_table.shape
    past = n_pages * PAGE_SIZE
    chunk = min(past, 1024)
    grid_spec = pltpu.PrefetchScalarGridSpec(
        num_scalar_prefetch=1,
        grid=(nb,),
        in_specs=[
            pl.BlockSpec((None, MLA_HEADS, KV_LORA), lambda b, pt: (b, 0, 0)),
            pl.BlockSpec((None, MLA_HEADS, ROPE_D), lambda b, pt: (b, 0, 0)),
            pl.BlockSpec((None, 1, KV_LORA), lambda b, pt: (b, 0, 0)),
            pl.BlockSpec((None, 1, ROPE_D), lambda b, pt: (b, 0, 0)),
            pl.BlockSpec(memory_space=pl.ANY),
            pl.BlockSpec(memory_space=pl.ANY),
        ],
        out_specs=pl.BlockSpec((None, MLA_HEADS, KV_LORA), lambda b, pt: (b, 0, 0)),
        scratch_shapes=[
            pltpu.VMEM((2, past, KV_LORA), F32),
            pltpu.VMEM((2, ROPE_D, past), F32),
            pltpu.VMEM((past, KV_LORA), BF16),
            pltpu.SemaphoreType.DMA((2, 2)),
        ],
    )
    return pl.pallas_call(
        functools.partial(_sample_attn_kernel, layer_j=layer_j, n_pages=n_pages, chunk=chunk),
        grid_spec=grid_spec,
        out_shape=jax.ShapeDtypeStruct((nb, MLA_HEADS, KV_LORA), BF16),
        compiler_params=_cp(("arbitrary",)),
        name="mla_sample_attn",
    )(page_table, q_lat, q_rope, ckv_new.reshape(nb, 1, KV_LORA), kr_new.reshape(nb, 1, ROPE_D),
      cache_ckv, cache_krope_t)


def _mla_out_kernel(ctx_ref, wuv_ref, wo_ref, x_ref, gate_ref, o_ref):
    parts = [_dot(ctx_ref[:, h * KV_LORA:(h + 1) * KV_LORA], wuv_ref[h]).astype(BF16)
             for h in range(MLA_HEADS)]
    o = jnp.concatenate(parts, axis=1)
    o_ref[...] = x_ref[...] + gate_ref[...] * _dot(o, wo_ref[...])


def _mla_out(ctx, w_uv_h, w_o, x, mod, layer, tm):
    m, d = x.shape
    return pl.pallas_call(
        _mla_out_kernel,
        grid=(m // tm,),
        in_specs=[
            pl.BlockSpec((tm, MLA_HEADS * KV_LORA), lambda i: (i, 0)),
            pl.BlockSpec(w_uv_h.shape, lambda i: (0, 0, 0)),
            pl.BlockSpec(w_o.shape, lambda i: (0, 0)),
            pl.BlockSpec((tm, d), lambda i: (i, 0)),
            mod.spec(layer, 2),
        ],
        out_specs=pl.BlockSpec((tm, d), lambda i: (i, 0)),
        out_shape=jax.ShapeDtypeStruct((m, d), F32),
        compiler_params=_cp(("parallel",)),
        name="mla_out_proj",
    )(ctx, w_uv_h, w_o, x, mod.arr)


def _tile(m, pref):
    t = min(m, pref)
    assert m % t == 0, (m, t)
    return t


def kernel(x_prompt, x_sample, c_prompt, c_sample, cache_conv, state_hgrn, cache_ckv, cache_krope,
           page_table, w_ada, b_ada, norm_mix, norm_ffn, w_in_even, conv_w, hg_lb_logits, hg_gnorm,
           w_out_even, w_dqkv, q_norm, w_uq, kv_norm, w_uk, w_uv, w_o, w_gu, w_down, norm_final):
    bp, tp, d = x_prompt.shape
    bs, ts, _ = x_sample.shape
    assert ts == 1 and d == D_MODEL
    depth = w_ada.shape[0]
    n_pages = page_table.shape[1]
    past_len = n_pages * PAGE_SIZE
    mp, ms = bp * tp, bs

    tm_p = _tile(tp, 512)
    tm_s = ms
    tt = _tile(tp, 256)
    tq = _tile(tp, 512)
    tb = _tile(bs, 16)
    krope_t = jnp.swapaxes(cache_krope, 2, 3)
    tf = 1408 if w_down.shape[1] % 1408 == 0 else LANES

    mods = _ada_all(jnp.concatenate([c_prompt, c_sample], axis=0), w_ada, b_ada)
    mod_p = _Mod(mods[:, :bp].reshape(depth, bp, 1, N_MOD * d), tp // tm_p)
    mod_s = _Mod(mods[:, bp:].reshape(depth, 1, bs, N_MOD * d), 1)

    cos_p, sin_p = _rope_table(tp, 0, 1, _tile(tp, 512))
    cos_s, sin_s = _rope_table(8, past_len, 0, 8)
    cos_s = jnp.broadcast_to(cos_s[:1], (ms, LANES))
    sin_s = jnp.broadcast_to(sin_s[:1], (ms, LANES))

    xp = x_prompt.reshape(mp, d)
    xs = x_sample.reshape(ms, d)
    outs = {k: [] for k in ("conv_p", "hg_p", "ckv_p", "kr_p", "conv_s", "hg_s", "ckv_s", "kr_s")}

    for l in range(depth):
        j = l // 2
        g_mix = norm_mix[l].reshape(1, d)
        g_ffn = norm_ffn[l].reshape(1, d)
        if l % 2 == 0:
            w_in = w_in_even[j].astype(BF16)
            w_out = w_out_even[j].astype(BF16)
            gn = hg_gnorm[j].reshape(1, HG_DK)
            zp = _mod_matmul(xp, g_mix, mod_p, l, w_in, tm_p, EVEN_IN // 2)
            zs = _mod_matmul(xs, g_mix, mod_s, l, w_in, tm_s, EVEN_IN // 2)
            abp, cvp, hgp = _prompt_mixer(zp, conv_w[j], hg_lb_logits, gn, j, bp, tp, tt)
            abs_, cvs, hgs = _sample_mixer(zs, cache_conv[j], state_hgrn[j], conv_w[j], hg_lb_logits,
                                           gn, j, tb)
            xp = _res_matmul(abp, w_out, xp, mod_p, l, 2, tm_p, D_MODEL)
            xs = _res_matmul(abs_, w_out, xs, mod_s, l, 2, tm_s, D_MODEL)
            outs["conv_p"].append(cvp)
            outs["hg_p"].append(hgp)
            outs["conv_s"].append(cvs)
            outs["hg_s"].append(hgs)
        else:
            w_d = jnp.pad(w_dqkv[j], ((0, 0), (0, ROPE_D))).astype(BF16)
            wq = w_uq[j].reshape(Q_LORA, MLA_HEADS, NOPE + ROPE_D)
            w_nope = wq[:, :, :NOPE].reshape(Q_LORA, MLA_HEADS * NOPE).astype(BF16)
            w_rope = wq[:, :, NOPE:].reshape(Q_LORA, MLA_HEADS * ROPE_D).astype(BF16)
            w_ukt = jnp.transpose(w_uk[j], (1, 2, 0)).astype(BF16)
            w_uvh = jnp.transpose(w_uv[j], (1, 0, 2)).astype(BF16)
            w_oj = w_o[j].astype(BF16)
            qn = q_norm[j].reshape(1, Q_LORA)
            kvn = kv_norm[j].reshape(1, KV_LORA)

            cqp, ckvp, krp, ckvpb, krpb = _dqkv(xp, g_mix, mod_p, l, w_d, qn, kvn, cos_p, sin_p,
                                                tm_p, tp // tm_p)
            cqs, ckvs, krs, _, _ = _dqkv(xs, g_mix, mod_s, l, w_d, qn, kvn, cos_s, sin_s, tm_s, 1)
            qlp, qrp = _q_proj(cqp, w_nope, w_rope, w_ukt, cos_p, sin_p, bp, tp, tm_p, tp // tm_p)
            qls, qrs = _q_proj(cqs, w_nope, w_rope, w_ukt, cos_s, sin_s, 1, ms, tm_s, 1)
            ctxp = _prompt_attn(qlp, qrp, ckvpb, krpb, bp, tp, tq)
            ctxs = _sample_attn(page_table,
                                jnp.transpose(qls[0], (1, 0, 2)), jnp.transpose(qrs[0], (1, 0, 2)),
                                ckvs, krs, cache_ckv, krope_t, j)
            xp = _mla_out(ctxp, w_uvh, w_oj, xp, mod_p, l, tm_p)
            xs = _mla_out(ctxs.reshape(ms, MLA_HEADS * KV_LORA), w_uvh, w_oj, xs, mod_s, l, tm_s)
            outs["ckv_p"].append(ckvp.reshape(bp, tp, KV_LORA))
            outs["kr_p"].append(krp.reshape(bp, tp, ROPE_D))
            outs["ckv_s"].append(ckvs.reshape(bs, ts, KV_LORA))
            outs["kr_s"].append(krs.reshape(bs, ts, ROPE_D))

        w_gu_l = w_gu[l].astype(BF16)
        w_dn_l = w_down[l].astype(BF16)
        last = l == depth - 1
        gfin = norm_final.reshape(1, d)
        xp = _ffn(xp, g_ffn, mod_p, l, w_gu_l, w_dn_l, gfin, last, tm_p, tf)
        xs = _ffn(xs, g_ffn, mod_s, l, w_gu_l, w_dn_l, gfin, last, tm_s, tf)

    return (xp.reshape(bp, tp, d), xs.reshape(bs, ts, d),
            jnp.stack(outs["conv_p"]), jnp.stack(outs["hg_p"]),
            jnp.stack(outs["ckv_p"]), jnp.stack(outs["kr_p"]),
            jnp.stack(outs["conv_s"]), jnp.stack(outs["hg_s"]),
            jnp.stack(outs["ckv_s"]), jnp.stack(outs["kr_s"]))
```

```python
import functools
import math

import jax
import jax.numpy as jnp
from jax import lax
from jax.experimental import pallas as pl
from jax.experimental.pallas import tpu as pltpu

F32 = jnp.float32
BF16 = jnp.bfloat16

D_MODEL = 1024
N_MOD = 6
EPS = 1e-6
NEG = -1e30
PAGE_SIZE = 128
CONV_CH = D_MODEL // 2
CONV_W = 3
HG_WIDTH = D_MODEL // 2
HG_DK = 128
HG_HEADS = HG_WIDTH // HG_DK
EVEN_IN = 3 * CONV_CH + 4 * HG_WIDTH
MLA_HEADS = 8
NOPE = 128
ROPE_D = 64
V_D = 128
Q_LORA = 384
KV_LORA = 256
ROPE_THETA = 10000.0
MLA_SCALE = (NOPE + ROPE_D) ** -0.5
LANES = 128
SUBLANES = 8
LOG2E = 1.0 / math.log(2.0)

Z_B, Z_C, Z_X = 0, CONV_CH, 2 * CONV_CH
Z_Q = 3 * CONV_CH
Z_F = Z_Q + HG_WIDTH
Z_I = Z_F + HG_WIDTH
Z_G = Z_I + HG_WIDTH

HG_SUB = 16
VMEM_LIMIT = 56 * 1024 * 1024


def _cp(sem, vmem=VMEM_LIMIT):
    return pltpu.CompilerParams(dimension_semantics=sem, vmem_limit_bytes=vmem)


def _silu(x):
    return x * jax.nn.sigmoid(x)


def _rmsnorm(x, g):
    return x * lax.rsqrt(jnp.mean(x * x, axis=-1, keepdims=True) + EPS) * g


def _dot(a, b):
    return jnp.dot(a, b, preferred_element_type=F32)


def _dot_nt(a, b):
    return lax.dot_general(a, b, (((1,), (1,)), ((), ())), preferred_element_type=F32)


def _dot_tn(a, b):
    return lax.dot_general(a, b, (((0,), (0,)), ((), ())), preferred_element_type=F32)


def _ada_kernel(c_ref, w_ref, b_ref, o_ref):
    a = _silu(c_ref[...]).astype(BF16)
    o_ref[...] = _dot(a, w_ref[...].astype(BF16)) + b_ref[...]


def _ada_all(c_all, w_ada, b_ada, tn=1536):
    depth, d, n6 = w_ada.shape
    rows = c_all.shape[0]
    return pl.pallas_call(
        _ada_kernel,
        grid=(depth, n6 // tn),
        in_specs=[
            pl.BlockSpec((rows, d), lambda l, j: (0, 0)),
            pl.BlockSpec((None, d, tn), lambda l, j: (l, 0, j)),
            pl.BlockSpec((None, 1, tn), lambda l, j: (l, 0, j)),
        ],
        out_specs=pl.BlockSpec((None, rows, tn), lambda l, j: (l, 0, j)),
        out_shape=jax.ShapeDtypeStruct((depth, rows, n6), F32),
        compiler_params=_cp(("parallel", "parallel")),
        name="adaln_mod",
    )(c_all, w_ada, b_ada.reshape(depth, 1, n6))


class _Mod:
    def __init__(self, arr, tps):
        self.arr = arr
        self.tps = tps
        self.r = arr.shape[2]

    def spec(self, layer, col, width=D_MODEL, ncol=None):
        tps = self.tps
        per = D_MODEL // width
        if ncol is None:
            return pl.BlockSpec((None, None, self.r, width),
                                lambda i, *_: (layer, i // tps, 0, col * per))
        return pl.BlockSpec((None, None, self.r, width),
                            lambda i, j, *_: (layer, i // tps, 0, col * per + j))


def _resident(shape):
    return pl.BlockSpec(shape, lambda *_: (0,) * len(shape), pipeline_mode=pl.Buffered(1))


def _modmm_kernel(x_ref, g_ref, sh_ref, sc_ref, w_ref, o_ref):
    h = _rmsnorm(x_ref[...], g_ref[...]) * (1.0 + sc_ref[...]) + sh_ref[...]
    o_ref[...] = _dot(h.astype(BF16), w_ref[...])


def _mod_matmul(x, g, mod, layer, w, tm):
    m, d = x.shape
    n = w.shape[1]
    return pl.pallas_call(
        _modmm_kernel,
        grid=(m // tm,),
        in_specs=[
            pl.BlockSpec((tm, d), lambda i: (i, 0)),
            pl.BlockSpec((1, d), lambda i: (0, 0)),
            mod.spec(layer, 0),
            mod.spec(layer, 1),
            _resident(w.shape),
        ],
        out_specs=pl.BlockSpec((tm, n), lambda i: (i, 0)),
        out_shape=jax.ShapeDtypeStruct((m, n), F32),
        compiler_params=_cp(("parallel",)),
        name="mod_matmul",
    )(x, g, mod.arr, mod.arr, w)


def _resmm_kernel(a_ref, w_ref, x_ref, gate_ref, o_ref):
    o_ref[...] = x_ref[...] + gate_ref[...] * _dot(a_ref[...], w_ref[...])


def _res_matmul(a, w, x, mod, layer, gate_col, tm, tn):
    m, k = a.shape
    n = w.shape[1]
    return pl.pallas_call(
        _resmm_kernel,
        grid=(m // tm, n // tn),
        in_specs=[
            pl.BlockSpec((tm, k), lambda i, j: (i, 0)),
            pl.BlockSpec((k, tn), lambda i, j: (0, j)),
            pl.BlockSpec((tm, tn), lambda i, j: (i, j)),
            mod.spec(layer, gate_col, width=tn, ncol=True),
        ],
        out_specs=pl.BlockSpec((tm, tn), lambda i, j: (i, j)),
        out_shape=jax.ShapeDtypeStruct((m, n), F32),
        compiler_params=_cp(("parallel", "parallel")),
        name="res_matmul",
    )(a, w, x, mod.arr)


def _ffn_kernel(x_ref, g_ref, sh_ref, sc_ref, gate_ref, wgu_ref, wd_ref, gf_ref, o_ref, a_scr,
                *, final_norm, dff, fc):
    x = x_ref[...]
    h = (_rmsnorm(x, g_ref[...]) * (1.0 + sc_ref[...]) + sh_ref[...]).astype(BF16)
    for c in range(dff // fc):
        gg = _dot(h, wgu_ref[:, c * fc:(c + 1) * fc])
        uu = _dot(h, wgu_ref[:, dff + c * fc:dff + (c + 1) * fc])
        a_scr[:, c * fc:(c + 1) * fc] = (_silu(gg) * uu).astype(BF16)
    y = x + gate_ref[...] * _dot(a_scr[...], wd_ref[...])
    if final_norm:
        y = _rmsnorm(y, gf_ref[...])
    o_ref[...] = y


def _ffn(x, g, mod, layer, w_gu, w_down, g_final, final_norm, tm, fc):
    m, d = x.shape
    dff = w_down.shape[0]
    return pl.pallas_call(
        functools.partial(_ffn_kernel, final_norm=final_norm, dff=dff, fc=fc),
        grid=(m // tm,),
        in_specs=[
            pl.BlockSpec((tm, d), lambda i: (i, 0)),
            pl.BlockSpec((1, d), lambda i: (0, 0)),
            mod.spec(layer, 3),
            mod.spec(layer, 4),
            mod.spec(layer, 5),
            _resident(w_gu.shape),
            _resident(w_down.shape),
            pl.BlockSpec((1, d), lambda i: (0, 0)),
        ],
        out_specs=pl.BlockSpec((tm, d), lambda i: (i, 0)),
        out_shape=jax.ShapeDtypeStruct((m, d), F32),
        scratch_shapes=[pltpu.VMEM((tm, dff), BF16)],
        compiler_params=_cp(("parallel",)),
        name="ffn",
    )(x, g, mod.arr, mod.arr, mod.arr, w_gu, w_down, g_final)


def _hg_lower_bound(lbl_ref, j):
    logits = lbl_ref[...]
    e = jnp.exp(logits - jnp.max(logits, axis=0, keepdims=True))
    den = jnp.sum(e, axis=0, keepdims=True)
    lb = jnp.zeros_like(den)
    for i in range(j):
        lb = lb + e[i:i + 1, :] / den
    return lb


def _cumsum_rows(x):
    rows = x.shape[0]
    idx = lax.broadcasted_iota(jnp.int32, x.shape, 0)
    d = 1
    while d < rows:
        x = x + jnp.where(idx >= d, pltpu.roll(x, d, 0), 0.0)
        d *= 2
    return x


def _hgrn_gates(qp, fp, lb):
    logf = jnp.log(lb + (1.0 - lb) * jax.nn.sigmoid(fp))
    kk = (1.0 - lb) * jax.nn.sigmoid(-fp)
    q = _silu(qp) * (HG_DK ** -0.5)
    return q, kk, logf


def _prompt_mixer_kernel(z_ref, cw_ref, lbl_ref, gn_ref, ab_ref, conv_ref, s_ref, ubuf, st_scr,
                         *, layer_j, tt):
    t = pl.program_id(1)
    nt = pl.num_programs(1)

    @pl.when(t == 0)
    def _():
        ubuf[0:8, :] = jnp.zeros((8, CONV_CH), F32)
        st_scr[...] = jnp.zeros_like(st_scr)

    u = z_ref[:, Z_C:Z_C + CONV_CH] * z_ref[:, Z_X:Z_X + CONV_CH]
    ubuf[8:8 + tt, :] = u
    y = (cw_ref[0:1, :] * ubuf[6:6 + tt, :] + cw_ref[1:2, :] * ubuf[7:7 + tt, :]
         + cw_ref[2:3, :] * ubuf[8:8 + tt, :])
    ab_ref[:, 0:CONV_CH] = (z_ref[:, Z_B:Z_B + CONV_CH] * y).astype(BF16)
    last2 = ubuf[tt + 6:tt + 8, :]
    ubuf[6:8, :] = last2
    conv_ref[...] = last2

    lb_all = _hg_lower_bound(lbl_ref, layer_j)
    gn = gn_ref[...]
    ell = HG_SUB
    row8 = lax.broadcasted_iota(jnp.int32, (SUBLANES, HG_DK), 0)

    def chunk(c, carry):
        r0 = pl.multiple_of(c * ell, ell)
        for h in range(HG_HEADS):
            lo = h * HG_DK
            lb = lb_all[:, lo:lo + HG_DK]
            qp = z_ref[pl.ds(r0, ell), Z_Q + lo:Z_Q + lo + HG_DK]
            fp = z_ref[pl.ds(r0, ell), Z_F + lo:Z_F + lo + HG_DK]
            v = z_ref[pl.ds(r0, ell), Z_I + lo:Z_I + lo + HG_DK]
            gp = z_ref[pl.ds(r0, ell), Z_G + lo:Z_G + lo + HG_DK]
            q, kk, logf = _hgrn_gates(qp, fp, lb)
            gc = _cumsum_rows(logf)
            gl = gc[ell - 1:ell, :]
            st = st_scr[h]
            o = _dot_nt((q * jnp.exp(gc)).astype(BF16), st.astype(BF16))
            o_grp = [o[r:r + SUBLANES, :] for r in range(0, ell, SUBLANES)]
            gc2 = gc * LOG2E
            for s in range(ell):
                for gi, r in enumerate(range(0, ell, SUBLANES)):
                    if r + SUBLANES <= s:
                        continue
                    d = gc2[r:r + SUBLANES, :] - gc2[s:s + 1, :]
                    if r <= s:
                        d = jnp.where(row8 >= s - r, d, NEG)
                    w = q[r:r + SUBLANES, :] * kk[s:s + 1, :] * jnp.exp2(d)
                    o_grp[gi] = o_grp[gi] + jnp.sum(w, axis=-1, keepdims=True) * v[s:s + 1, :]
            o = jnp.concatenate(o_grp, axis=0)
            kd = kk * jnp.exp(gl - gc)
            st_scr[h] = st * jnp.exp(gl) + _dot_tn(v.astype(BF16), kd.astype(BF16))
            b = _rmsnorm(o, gn) * _silu(gp)
            ab_ref[pl.ds(r0, ell), CONV_CH + lo:CONV_CH + lo + HG_DK] = b.astype(BF16)
        return carry

    lax.fori_loop(0, tt // ell, chunk, 0, unroll=2)

    @pl.when(t == nt - 1)
    def _():
        for h in range(HG_HEADS):
            s_ref[h] = st_scr[h].T


def _prompt_mixer(z, conv_w_j, lb_logits, gnorm_j, layer_j, nb, seq, tt):
    m = z.shape[0]
    nt = seq // tt
    return pl.pallas_call(
        functools.partial(_prompt_mixer_kernel, layer_j=layer_j, tt=tt),
        grid=(nb, nt),
        in_specs=[
            pl.BlockSpec((tt, EVEN_IN), lambda n, t: (n * nt + t, 0)),
            pl.BlockSpec((CONV_W, CONV_CH), lambda n, t: (0, 0)),
            pl.BlockSpec(lb_logits.shape, lambda n, t: (0, 0)),
            pl.BlockSpec((1, HG_DK), lambda n, t: (0, 0)),
        ],
        out_specs=[
            pl.BlockSpec((tt, D_MODEL), lambda n, t: (n * nt + t, 0)),
            pl.BlockSpec((None, CONV_W - 1, CONV_CH), lambda n, t: (n, 0, 0)),
            pl.BlockSpec((None, HG_HEADS, HG_DK, HG_DK), lambda n, t: (n, 0, 0, 0)),
        ],
        out_shape=[
            jax.ShapeDtypeStruct((m, D_MODEL), BF16),
            jax.ShapeDtypeStruct((nb, CONV_W - 1, CONV_CH), F32),
            jax.ShapeDtypeStruct((nb, HG_HEADS, HG_DK, HG_DK), F32),
        ],
        scratch_shapes=[pltpu.VMEM((tt + 8, CONV_CH), F32),
                        pltpu.VMEM((HG_HEADS, HG_DK, HG_DK), F32)],
        compiler_params=_cp(("parallel", "arbitrary")),
        name="prompt_conv_hgrn",
    )(z, conv_w_j, lb_logits, gnorm_j)


def _column(row_vec, eye):
    return jnp.sum(jnp.where(eye, row_vec, 0.0), axis=1, keepdims=True)


def _sample_mixer_kernel(z_ref, cb_ref, s0_ref, cw_ref, lbl_ref, gn_ref, ab_ref, conv_ref, s_ref,
                         b_scr, *, layer_j, tb):
    u = z_ref[:, Z_C:Z_C + CONV_CH] * z_ref[:, Z_X:Z_X + CONV_CH]
    b0 = cb_ref[:, 0, :]
    b1 = cb_ref[:, 1, :]
    y = cw_ref[0:1, :] * b0 + cw_ref[1:2, :] * b1 + cw_ref[2:3, :] * u
    ab_ref[:, 0:CONV_CH] = (z_ref[:, Z_B:Z_B + CONV_CH] * y).astype(BF16)
    conv_ref[:, 0, :] = b1
    conv_ref[:, 1, :] = u

    lb_all = _hg_lower_bound(lbl_ref, layer_j)
    gn = gn_ref[...]
    eye = (lax.broadcasted_iota(jnp.int32, (HG_DK, HG_DK), 0)
           == lax.broadcasted_iota(jnp.int32, (HG_DK, HG_DK), 1))

    q_all, kk_all, logf_all = _hgrn_gates(z_ref[:, Z_Q:Z_Q + HG_WIDTH], z_ref[:, Z_F:Z_F + HG_WIDTH],
                                          lb_all)
    ef_all = jnp.exp(logf_all)
    v_all = z_ref[:, Z_I:Z_I + HG_WIDTH]
    for n in range(tb):
        for h in range(HG_HEADS):
            lo = h * HG_DK
            row = lambda a: a[n:n + 1, lo:lo + HG_DK]
            s_new = (_column(row(ef_all), eye) * s0_ref[n, h]
                     + _column(row(kk_all), eye) * row(v_all))
            s_ref[n, h] = s_new
            b_scr[n:n + 1, lo:lo + HG_DK] = jnp.sum(_column(row(q_all), eye) * s_new, axis=0,
                                                    keepdims=True)
    for h in range(HG_HEADS):
        lo = h * HG_DK
        b = _rmsnorm(b_scr[:, lo:lo + HG_DK], gn) * _silu(z_ref[:, Z_G + lo:Z_G + lo + HG_DK])
        ab_ref[:, CONV_CH + lo:CONV_CH + lo + HG_DK] = b.astype(BF16)


def _sample_mixer(z, conv_buf, s0, conv_w_j, lb_logits, gnorm_j, layer_j, tb):
    nb = z.shape[0]
    return pl.pallas_call(
        functools.partial(_sample_mixer_kernel, layer_j=layer_j, tb=tb),
        grid=(nb // tb,),
        in_specs=[
            pl.BlockSpec((tb, EVEN_IN), lambda i: (i, 0)),
            pl.BlockSpec((tb, CONV_W - 1, CONV_CH), lambda i: (i, 0, 0)),
            pl.BlockSpec((tb, HG_HEADS, HG_DK, HG_DK), lambda i: (i, 0, 0, 0)),
            pl.BlockSpec((CONV_W, CONV_CH), lambda i: (0, 0)),
            pl.BlockSpec(lb_logits.shape, lambda i: (0, 0)),
            pl.BlockSpec((1, HG_DK), lambda i: (0, 0)),
        ],
        out_specs=[
            pl.BlockSpec((tb, D_MODEL), lambda i: (i, 0)),
            pl.BlockSpec((tb, CONV_W - 1, CONV_CH), lambda i: (i, 0, 0)),
            pl.BlockSpec((tb, HG_HEADS, HG_DK, HG_DK), lambda i: (i, 0, 0, 0)),
        ],
        out_shape=[
            jax.ShapeDtypeStruct((nb, D_MODEL), BF16),
            jax.ShapeDtypeStruct((nb, CONV_W - 1, CONV_CH), F32),
            jax.ShapeDtypeStruct((nb, HG_HEADS, HG_DK, HG_DK), F32),
        ],
        scratch_shapes=[pltpu.VMEM((tb, HG_WIDTH), F32)],
        compiler_params=_cp(("parallel",)),
        name="sample_conv_hgrn",
    )(z, conv_buf, s0, conv_w_j, lb_logits, gnorm_j)


def _rope_table_kernel(cos_ref, sin_ref, *, tr, pos0, step):
    lane = lax.broadcasted_iota(jnp.int32, (tr, LANES), 1)
    rowi = lax.broadcasted_iota(jnp.int32, (tr, LANES), 0)
    half = ROPE_D // 2
    fi = (lane % half).astype(F32)
    inv = jnp.exp(fi * (-math.log(ROPE_THETA) / half))
    pos = (pos0 + step * (pl.program_id(0) * tr + rowi)).astype(F32)
    ang = pos * inv
    sign = jnp.where((lane % ROPE_D) < half, -1.0, 1.0)
    cos_ref[...] = jnp.cos(ang)
    sin_ref[...] = jnp.sin(ang) * sign


def _rope_table(rows, pos0, step, tr):
    return pl.pallas_call(
        functools.partial(_rope_table_kernel, tr=tr, pos0=pos0, step=step),
        grid=(rows // tr,),
        out_specs=[pl.BlockSpec((tr, LANES), lambda i: (i, 0))] * 2,
        out_shape=[jax.ShapeDtypeStruct((rows, LANES), F32)] * 2,
        compiler_params=_cp(("parallel",)),
        name="rope_table",
    )()


def _rope_pairs(g, cos, sin_signed):
    lane = lax.broadcasted_iota(jnp.int32, g.shape, 1)
    half = ROPE_D // 2
    n = g.shape[1]
    rot = jnp.where((lane % ROPE_D) < half, pltpu.roll(g, n - half, 1), pltpu.roll(g, half, 1))
    return g * cos + rot * sin_signed


def _dqkv_kernel(x_ref, g_ref, sh_ref, sc_ref, w_ref, qn_ref, kvn_ref, cos_ref, sin_ref,
                 cq_ref, ckv_ref, kr_ref, ckvb_ref, krb_ref):
    h = _rmsnorm(x_ref[...], g_ref[...]) * (1.0 + sc_ref[...]) + sh_ref[...]
    d = _dot(h.astype(BF16), w_ref[...])
    cq_ref[...] = _rmsnorm(d[:, :Q_LORA], qn_ref[...]).astype(BF16)
    ckv = _rmsnorm(d[:, Q_LORA:Q_LORA + KV_LORA], kvn_ref[...])
    ckv_ref[...] = ckv
    ckvb_ref[...] = ckv.astype(BF16)
    kr = _rope_pairs(d[:, Q_LORA + KV_LORA:], cos_ref[...], sin_ref[...])[:, :ROPE_D]
    kr_ref[...] = kr
    krb_ref[...] = kr.astype(BF16)


def _dqkv(x, g, mod, layer, w_pad, q_norm, kv_norm, cos_t, sin_t, tm, rope_blocks):
    m, d = x.shape
    n = w_pad.shape[1]
    rb = rope_blocks
    return pl.pallas_call(
        _dqkv_kernel,
        grid=(m // tm,),
        in_specs=[
            pl.BlockSpec((tm, d), lambda i: (i, 0)),
            pl.BlockSpec((1, d), lambda i: (0, 0)),
            mod.spec(layer, 0),
            mod.spec(layer, 1),
            pl.BlockSpec((d, n), lambda i: (0, 0)),
            pl.BlockSpec((1, Q_LORA), lambda i: (0, 0)),
            pl.BlockSpec((1, KV_LORA), lambda i: (0, 0)),
            pl.BlockSpec((cos_t.shape[0] // rb, LANES), lambda i: (i % rb, 0)),
            pl.BlockSpec((cos_t.shape[0] // rb, LANES), lambda i: (i % rb, 0)),
        ],
        out_specs=[
            pl.BlockSpec((tm, Q_LORA), lambda i: (i, 0)),
            pl.BlockSpec((tm, KV_LORA), lambda i: (i, 0)),
            pl.BlockSpec((tm, ROPE_D), lambda i: (i, 0)),
            pl.BlockSpec((tm, KV_LORA), lambda i: (i, 0)),
            pl.BlockSpec((tm, ROPE_D), lambda i: (i, 0)),
        ],
        out_shape=[
            jax.ShapeDtypeStruct((m, Q_LORA), BF16),
            jax.ShapeDtypeStruct((m, KV_LORA), F32),
            jax.ShapeDtypeStruct((m, ROPE_D), F32),
            jax.ShapeDtypeStruct((m, KV_LORA), BF16),
            jax.ShapeDtypeStruct((m, ROPE_D), BF16),
        ],
        compiler_params=_cp(("parallel",)),
        name="mla_down_proj",
    )(x, g, mod.arr, mod.arr, w_pad, q_norm, kv_norm, cos_t, sin_t)


def _q_kernel(cq_ref, wn_ref, wr_ref, wuk_ref, cos_ref, sin_ref, ql_ref, qr_ref):
    cq = cq_ref[...]
    qn = _dot(cq, wn_ref[...])
    qr = _dot(cq, wr_ref[...])
    cos = jnp.concatenate([cos_ref[...]] * (MLA_HEADS * ROPE_D // LANES), axis=1)
    sin = jnp.concatenate([sin_ref[...]] * (MLA_HEADS * ROPE_D // LANES), axis=1)
    qr = _rope_pairs(qr, cos, sin).astype(BF16)
    for h in range(MLA_HEADS):
        ql_ref[h] = _dot(qn[:, h * NOPE:(h + 1) * NOPE].astype(BF16), wuk_ref[h]).astype(BF16)
        qr_ref[h] = qr[:, h * ROPE_D:(h + 1) * ROPE_D]


def _q_proj(cq, w_nope, w_rope, w_ukt, cos_t, sin_t, nb, seq, tm, rope_blocks):
    nt = seq // tm
    rb = rope_blocks
    return pl.pallas_call(
        _q_kernel,
        grid=(nb, nt),
        in_specs=[
            pl.BlockSpec((tm, Q_LORA), lambda n, t: (n * nt + t, 0)),
            pl.BlockSpec(w_nope.shape, lambda n, t: (0, 0)),
            pl.BlockSpec(w_rope.shape, lambda n, t: (0, 0)),
            pl.BlockSpec(w_ukt.shape, lambda n, t: (0, 0, 0)),
            pl.BlockSpec((cos_t.shape[0] // rb, LANES), lambda n, t: (t % rb, 0)),
            pl.BlockSpec((cos_t.shape[0] // rb, LANES), lambda n, t: (t % rb, 0)),
        ],
        out_specs=[
            pl.BlockSpec((None, MLA_HEADS, tm, KV_LORA), lambda n, t: (n, 0, t, 0)),
            pl.BlockSpec((None, MLA_HEADS, tm, ROPE_D), lambda n, t: (n, 0, t, 0)),
        ],
        out_shape=[
            jax.ShapeDtypeStruct((nb, MLA_HEADS, seq, KV_LORA), BF16),
            jax.ShapeDtypeStruct((nb, MLA_HEADS, seq, ROPE_D), BF16),
        ],
        compiler_params=_cp(("parallel", "parallel")),
        name="mla_q_proj",
    )(cq, w_nope, w_rope, w_ukt, cos_t, sin_t)


def _lane_tile(x, width):
    return x if width == LANES else jnp.concatenate([x] * (width // LANES), axis=1)


def _prompt_attn_kernel(ql_ref, qr_ref, k_ref, kr_ref, o_ref, m_scr, l_scr, acc_scr, *, tq):
    i = pl.program_id(1)
    m_scr[...] = jnp.full_like(m_scr, NEG)
    l_scr[...] = jnp.zeros_like(l_scr)
    acc_scr[...] = jnp.zeros_like(acc_scr)

    def block(j, masked):
        r0 = pl.multiple_of(j * tq, tq)
        k = k_ref[pl.ds(r0, tq), :]
        kr = kr_ref[pl.ds(r0, tq), :]
        if masked:
            causal = (lax.broadcasted_iota(jnp.int32, (tq, tq), 1)
                      <= lax.broadcasted_iota(jnp.int32, (tq, tq), 0))

        def head(h, carry):
            s = (_dot_nt(ql_ref[h], k) + _dot_nt(qr_ref[h], kr)) * (MLA_SCALE * LOG2E)
            if masked:
                s = jnp.where(causal, s, NEG)
            m_prev = m_scr[h]
            m_new = jnp.maximum(m_prev, jnp.max(s, axis=-1, keepdims=True))
            alpha = jnp.exp2(m_prev - m_new)
            p = jnp.exp2(s - _lane_tile(m_new, tq))
            l_scr[h] = alpha * l_scr[h] + jnp.sum(p, axis=-1, keepdims=True)
            acc_scr[h] = _lane_tile(alpha, KV_LORA) * acc_scr[h] + _dot(p.astype(BF16), k)
            m_scr[h] = m_new
            return carry

        lax.fori_loop(0, MLA_HEADS, head, 0, unroll=True)

    def body(j, carry):
        block(j, False)
        return carry

    lax.fori_loop(0, i, body, 0)
    block(i, True)
    for h in range(MLA_HEADS):
        ctx = acc_scr[h] / _lane_tile(l_scr[h], KV_LORA)
        o_ref[:, h * KV_LORA:(h + 1) * KV_LORA] = ctx.astype(BF16)


def _prompt_attn(q_lat, q_rope, ckv_b, kr_b, nb, seq, tq):
    nq = seq // tq
    return pl.pallas_call(
        functools.partial(_prompt_attn_kernel, tq=tq),
        grid=(nb, nq),
        in_specs=[
            pl.BlockSpec((None, MLA_HEADS, tq, KV_LORA), lambda n, i: (n, 0, i, 0)),
            pl.BlockSpec((None, MLA_HEADS, tq, ROPE_D), lambda n, i: (n, 0, i, 0)),
            pl.BlockSpec((seq, KV_LORA), lambda n, i: (n, 0)),
            pl.BlockSpec((seq, ROPE_D), lambda n, i: (n, 0)),
        ],
        out_specs=pl.BlockSpec((tq, MLA_HEADS * KV_LORA), lambda n, i: (n * nq + i, 0)),
        out_shape=jax.ShapeDtypeStruct((nb * seq, MLA_HEADS * KV_LORA), BF16),
        scratch_shapes=[pltpu.VMEM((MLA_HEADS, tq, LANES), F32),
                        pltpu.VMEM((MLA_HEADS, tq, LANES), F32),
                        pltpu.VMEM((MLA_HEADS, tq, KV_LORA), F32)],
        compiler_params=_cp(("parallel", "arbitrary")),
        name="mla_prompt_attn",
    )(q_lat, q_rope, ckv_b, kr_b)


def _sample_attn_kernel(pt_ref, ql_ref, qr_ref, cn_ref, krn_ref, ckv_hbm, krt_hbm, o_ref,
                        kbuf, rbuf, kb16, sems, *, layer_j, n_pages, chunk):
    b = pl.program_id(0)
    nb = pl.num_programs(0)
    slot = b % 2
    past = n_pages * PAGE_SIZE

    ppc = chunk // PAGE_SIZE

    def page_copies(seq, slot_, p):
        page = pt_ref[seq, p]
        row0 = pl.multiple_of(p * PAGE_SIZE, PAGE_SIZE)
        c = p // ppc
        return (
            pltpu.make_async_copy(ckv_hbm.at[layer_j, page],
                                  kbuf.at[slot_, pl.ds(row0, PAGE_SIZE), :], sems.at[0, slot_, c]),
            pltpu.make_async_copy(krt_hbm.at[layer_j, page],
                                  rbuf.at[slot_, :, pl.ds(row0, PAGE_SIZE)], sems.at[1, slot_, c]),
        )

    @pl.when(b == 0)
    def _():
        def start(p, carry):
            for cp in page_copies(0, 0, p):
                cp.start()
            return carry
        lax.fori_loop(0, n_pages, start, 0, unroll=4)

    nxt = jnp.minimum(b + 1, nb - 1)
    ql = ql_ref[...]
    qr = qr_ref[...]
    scores = []
    for c in range(past // chunk):
        for p in range(c * ppc, (c + 1) * ppc):
            for cp in page_copies(nxt, 1 - slot, p):
                cp.start()
        for p in range(c * ppc, (c + 1) * ppc):
            for cp in page_copies(b, slot, p):
                cp.wait()
        kb = kbuf[slot, c * chunk:(c + 1) * chunk, :].astype(BF16)
        kb16[c * chunk:(c + 1) * chunk, :] = kb
        rb = rbuf[slot, :, c * chunk:(c + 1) * chunk].astype(BF16)
        scores.append((_dot_nt(ql, kb) + _dot(qr, rb)) * MLA_SCALE)
    cn = cn_ref[...]
    s_new = (jnp.sum(ql.astype(F32) * cn, axis=-1, keepdims=True)
             + jnp.sum(qr.astype(F32) * krn_ref[...], axis=-1, keepdims=True)) * MLA_SCALE
    m = s_new
    for s in scores:
        m = jnp.maximum(m, jnp.max(s, axis=-1, keepdims=True))
    p_new = jnp.exp(s_new - m)
    l = p_new
    acc = p_new * cn
    for c, s in enumerate(scores):
        p = jnp.exp(s - m)
        l = l + jnp.sum(p, axis=-1, keepdims=True)
        acc = acc + _dot(p.astype(BF16), kb16[c * chunk:(c + 1) * chunk, :])
    o_ref[...] = (acc / l).astype(BF16)

    @pl.when(b == nb - 1)
    def _():
        def drain(p, carry):
            for cp in page_copies(nxt, 1 - slot, p):
                cp.wait()
            return carry
        lax.fori_loop(0, n_pages, drain, 0, unroll=4)


def _sample_attn(page_table, q_lat, q_rope, ckv_new, kr_new, cache_ckv, cache_krope_t, layer_j):
    nb, n_pages = page_table.shape
    past = n_pages * PAGE_SIZE
    chunk = min(past, 1024)
    grid_spec = pltpu.PrefetchScalarGridSpec(
        num_scalar_prefetch=1,
        grid=(nb,),
        in_specs=[
            pl.BlockSpec((None, MLA_HEADS, KV_LORA), lambda b, pt: (b, 0, 0)),
            pl.BlockSpec((None, MLA_HEADS, ROPE_D), lambda b, pt: (b, 0, 0)),
            pl.BlockSpec((None, 1, KV_LORA), lambda b, pt: (b, 0, 0)),
            pl.BlockSpec((None, 1, ROPE_D), lambda b, pt: (b, 0, 0)),
            pl.BlockSpec(memory_space=pl.ANY),
            pl.BlockSpec(memory_space=pl.ANY),
        ],
        out_specs=pl.BlockSpec((None, MLA_HEADS, KV_LORA), lambda b, pt: (b, 0, 0)),
        scratch_shapes=[
            pltpu.VMEM((2, past, KV_LORA), F32),
            pltpu.VMEM((2, ROPE_D, past), F32),
            pltpu.VMEM((past, KV_LORA), BF16),
            pltpu.SemaphoreType.DMA((2, 2, past // chunk)),
        ],
    )
    return pl.pallas_call(
        functools.partial(_sample_attn_kernel, layer_j=layer_j, n_pages=n_pages, chunk=chunk),
        grid_spec=grid_spec,
        out_shape=jax.ShapeDtypeStruct((nb, MLA_HEADS, KV_LORA), BF16),
        compiler_params=_cp(("arbitrary",)),
        name="mla_sample_attn",
    )(page_table, q_lat, q_rope, ckv_new.reshape(nb, 1, KV_LORA), kr_new.reshape(nb, 1, ROPE_D),
      cache_ckv, cache_krope_t)


def _mla_out_kernel(ctx_ref, wuv_ref, wo_ref, x_ref, gate_ref, o_ref):
    parts = [_dot(ctx_ref[:, h * KV_LORA:(h + 1) * KV_LORA], wuv_ref[h]).astype(BF16)
             for h in range(MLA_HEADS)]
    o = jnp.concatenate(parts, axis=1)
    o_ref[...] = x_ref[...] + gate_ref[...] * _dot(o, wo_ref[...])


def _mla_out(ctx, w_uv_h, w_o, x, mod, layer, tm):
    m, d = x.shape
    return pl.pallas_call(
        _mla_out_kernel,
        grid=(m // tm,),
        in_specs=[
            pl.BlockSpec((tm, MLA_HEADS * KV_LORA), lambda i: (i, 0)),
            pl.BlockSpec(w_uv_h.shape, lambda i: (0, 0, 0)),
            pl.BlockSpec(w_o.shape, lambda i: (0, 0)),
            pl.BlockSpec((tm, d), lambda i: (i, 0)),
            mod.spec(layer, 2),
        ],
        out_specs=pl.BlockSpec((tm, d), lambda i: (i, 0)),
        out_shape=jax.ShapeDtypeStruct((m, d), F32),
        compiler_params=_cp(("parallel",)),
        name="mla_out_proj",
    )(ctx, w_uv_h, w_o, x, mod.arr)


def _tile(m, pref):
    t = min(m, pref)
    assert m % t == 0, (m, t)
    return t


def kernel(x_prompt, x_sample, c_prompt, c_sample, cache_conv, state_hgrn, cache_ckv, cache_krope,
           page_table, w_ada, b_ada, norm_mix, norm_ffn, w_in_even, conv_w, hg_lb_logits, hg_gnorm,
           w_out_even, w_dqkv, q_norm, w_uq, kv_norm, w_uk, w_uv, w_o, w_gu, w_down, norm_final):
    bp, tp, d = x_prompt.shape
    bs, ts, _ = x_sample.shape
    assert ts == 1 and d == D_MODEL
    depth = w_ada.shape[0]
    n_pages = page_table.shape[1]
    past_len = n_pages * PAGE_SIZE
    mp, ms = bp * tp, bs

    tm_p = _tile(tp, 512)
    tm_s = ms
    tt = _tile(tp, 256)
    tq = _tile(tp, 512)
    tb = _tile(bs, 16)
    krope_t = jnp.swapaxes(cache_krope, 2, 3)
    fc = 256

    mods = _ada_all(jnp.concatenate([c_prompt, c_sample], axis=0), w_ada, b_ada)
    mod_p = _Mod(mods[:, :bp].reshape(depth, bp, 1, N_MOD * d), tp // tm_p)
    mod_s = _Mod(mods[:, bp:].reshape(depth, 1, bs, N_MOD * d), 1)

    cos_p, sin_p = _rope_table(tp, 0, 1, _tile(tp, 512))
    cos_s, sin_s = _rope_table(8, past_len, 0, 8)
    cos_s = jnp.broadcast_to(cos_s[:1], (ms, LANES))
    sin_s = jnp.broadcast_to(sin_s[:1], (ms, LANES))

    xp = x_prompt.reshape(mp, d)
    xs = x_sample.reshape(ms, d)
    outs = {k: [] for k in ("conv_p", "hg_p", "ckv_p", "kr_p", "conv_s", "hg_s", "ckv_s", "kr_s")}

    for l in range(depth):
        j = l // 2
        g_mix = norm_mix[l].reshape(1, d)
        g_ffn = norm_ffn[l].reshape(1, d)
        if l % 2 == 0:
            w_in = w_in_even[j].astype(BF16)
            w_out = w_out_even[j].astype(BF16)
            gn = hg_gnorm[j].reshape(1, HG_DK)
            zp = _mod_matmul(xp, g_mix, mod_p, l, w_in, tm_p)
            zs = _mod_matmul(xs, g_mix, mod_s, l, w_in, tm_s)
            abp, cvp, hgp = _prompt_mixer(zp, conv_w[j], hg_lb_logits, gn, j, bp, tp, tt)
            abs_, cvs, hgs = _sample_mixer(zs, cache_conv[j], state_hgrn[j], conv_w[j], hg_lb_logits,
                                           gn, j, tb)
            xp = _res_matmul(abp, w_out, xp, mod_p, l, 2, tm_p, D_MODEL)
            xs = _res_matmul(abs_, w_out, xs, mod_s, l, 2, tm_s, D_MODEL)
            outs["conv_p"].append(cvp)
            outs["hg_p"].append(hgp)
            outs["conv_s"].append(cvs)
            outs["hg_s"].append(hgs)
        else:
            w_d = jnp.pad(w_dqkv[j], ((0, 0), (0, ROPE_D))).astype(BF16)
            wq = w_uq[j].reshape(Q_LORA, MLA_HEADS, NOPE + ROPE_D)
            w_nope = wq[:, :, :NOPE].reshape(Q_LORA, MLA_HEADS * NOPE).astype(BF16)
            w_rope = wq[:, :, NOPE:].reshape(Q_LORA, MLA_HEADS * ROPE_D).astype(BF16)
            w_ukt = jnp.transpose(w_uk[j], (1, 2, 0)).astype(BF16)
            w_uvh = jnp.transpose(w_uv[j], (1, 0, 2)).astype(BF16)
            w_oj = w_o[j].astype(BF16)
            qn = q_norm[j].reshape(1, Q_LORA)
            kvn = kv_norm[j].reshape(1, KV_LORA)

            cqp, ckvp, krp, ckvpb, krpb = _dqkv(xp, g_mix, mod_p, l, w_d, qn, kvn, cos_p, sin_p,
                                                tm_p, tp // tm_p)
            cqs, ckvs, krs, _, _ = _dqkv(xs, g_mix, mod_s, l, w_d, qn, kvn, cos_s, sin_s, tm_s, 1)
            qlp, qrp = _q_proj(cqp, w_nope, w_rope, w_ukt, cos_p, sin_p, bp, tp, tm_p, tp // tm_p)
            qls, qrs = _q_proj(cqs, w_nope, w_rope, w_ukt, cos_s, sin_s, 1, ms, tm_s, 1)
            ctxp = _prompt_attn(qlp, qrp, ckvpb, krpb, bp, tp, tq)
            ctxs = _sample_attn(page_table,
                                jnp.transpose(qls[0], (1, 0, 2)), jnp.transpose(qrs[0], (1, 0, 2)),
                                ckvs, krs, cache_ckv, krope_t, j)
            xp = _mla_out(ctxp, w_uvh, w_oj, xp, mod_p, l, tm_p)
            xs = _mla_out(ctxs.reshape(ms, MLA_HEADS * KV_LORA), w_uvh, w_oj, xs, mod_s, l, tm_s)
            outs["ckv_p"].append(ckvp.reshape(bp, tp, KV_LORA))
            outs["kr_p"].append(krp.reshape(bp, tp, ROPE_D))
            outs["ckv_s"].append(ckvs.reshape(bs, ts, KV_LORA))
            outs["kr_s"].append(krs.reshape(bs, ts, ROPE_D))

        w_gu_l = w_gu[l].astype(BF16)
        w_dn_l = w_down[l].astype(BF16)
        last = l == depth - 1
        gfin = norm_final.reshape(1, d)
        xp = _ffn(xp, g_ffn, mod_p, l, w_gu_l, w_dn_l, gfin, last, tm_p, fc)
        xs = _ffn(xs, g_ffn, mod_s, l, w_gu_l, w_dn_l, gfin, last, tm_s, fc)

    return (xp.reshape(bp, tp, d), xs.reshape(bs, ts, d),
            jnp.stack(outs["conv_p"]), jnp.stack(outs["hg_p"]),
            jnp.stack(outs["ckv_p"]), jnp.stack(outs["kr_p"]),
            jnp.stack(outs["conv_s"]), jnp.stack(outs["hg_s"]),
            jnp.stack(outs["ckv_s"]), jnp.stack(outs["kr_s"]))
```

```python
import functools
import math

import jax
import jax.numpy as jnp
from jax import lax
from jax.experimental import pallas as pl
from jax.experimental.pallas import tpu as pltpu

F32 = jnp.float32
BF16 = jnp.bfloat16

D_MODEL = 1024
N_MOD = 6
EPS = 1e-6
NEG = -1e30
PAGE_SIZE = 128
CONV_CH = D_MODEL // 2
CONV_W = 3
HG_WIDTH = D_MODEL // 2
HG_DK = 128
HG_HEADS = HG_WIDTH // HG_DK
EVEN_IN = 3 * CONV_CH + 4 * HG_WIDTH
MLA_HEADS = 8
NOPE = 128
ROPE_D = 64
V_D = 128
Q_LORA = 384
KV_LORA = 256
ROPE_THETA = 10000.0
MLA_SCALE = (NOPE + ROPE_D) ** -0.5
LANES = 128
SUBLANES = 8
LOG2E = 1.0 / math.log(2.0)

Z_B, Z_C, Z_X = 0, CONV_CH, 2 * CONV_CH
Z_Q = 3 * CONV_CH
Z_F = Z_Q + HG_WIDTH
Z_I = Z_F + HG_WIDTH
Z_G = Z_I + HG_WIDTH

HG_SUB = 16
VMEM_LIMIT = 56 * 1024 * 1024


def _cp(sem, vmem=VMEM_LIMIT):
    return pltpu.CompilerParams(dimension_semantics=sem, vmem_limit_bytes=vmem)


def _silu(x):
    return x * jax.nn.sigmoid(x)


def _rmsnorm(x, g):
    return x * lax.rsqrt(jnp.mean(x * x, axis=-1, keepdims=True) + EPS) * g


def _dot(a, b):
    return jnp.dot(a, b, preferred_element_type=F32)


def _dot_nt(a, b):
    return lax.dot_general(a, b, (((1,), (1,)), ((), ())), preferred_element_type=F32)


def _dot_tn(a, b):
    return lax.dot_general(a, b, (((0,), (0,)), ((), ())), preferred_element_type=F32)


def _ada_kernel(c_ref, w_ref, b_ref, o_ref):
    a = _silu(c_ref[...]).astype(BF16)
    o_ref[...] = _dot(a, w_ref[...].astype(BF16)) + b_ref[...]


def _ada_all(c_all, w_ada, b_ada, tn=1536):
    depth, d, n6 = w_ada.shape
    rows = c_all.shape[0]
    return pl.pallas_call(
        _ada_kernel,
        grid=(depth, n6 // tn),
        in_specs=[
            pl.BlockSpec((rows, d), lambda l, j: (0, 0)),
            pl.BlockSpec((None, d, tn), lambda l, j: (l, 0, j)),
            pl.BlockSpec((None, 1, tn), lambda l, j: (l, 0, j)),
        ],
        out_specs=pl.BlockSpec((None, rows, tn), lambda l, j: (l, 0, j)),
        out_shape=jax.ShapeDtypeStruct((depth, rows, n6), F32),
        compiler_params=_cp(("parallel", "parallel")),
        name="adaln_mod",
    )(c_all, w_ada, b_ada.reshape(depth, 1, n6))


class _Mod:
    def __init__(self, arr, tps):
        self.arr = arr
        self.tps = tps
        self.r = arr.shape[2]

    def spec(self, layer, col, width=D_MODEL, ncol=None):
        tps = self.tps
        per = D_MODEL // width
        if ncol is None:
            return pl.BlockSpec((None, None, self.r, width),
                                lambda i, *_: (layer, i // tps, 0, col * per))
        return pl.BlockSpec((None, None, self.r, width),
                            lambda i, j, *_: (layer, i // tps, 0, col * per + j))


def _resident(stacked, layer):
    shape = stacked.shape[1:]
    return pl.BlockSpec((None,) + shape, lambda *_: (layer,) + (0,) * len(shape),
                        pipeline_mode=pl.Buffered(1))


def _modmm_kernel(x_ref, g_ref, sh_ref, sc_ref, w_ref, o_ref):
    h = _rmsnorm(x_ref[...], g_ref[...]) * (1.0 + sc_ref[...]) + sh_ref[...]
    o_ref[...] = _dot(h.astype(BF16), w_ref[...])


def _mod_matmul(x, g, mod, layer, w_all, j, tm):
    m, d = x.shape
    n = w_all.shape[2]
    return pl.pallas_call(
        _modmm_kernel,
        grid=(m // tm,),
        in_specs=[
            pl.BlockSpec((tm, d), lambda i: (i, 0)),
            pl.BlockSpec((1, d), lambda i: (0, 0)),
            mod.spec(layer, 0),
            mod.spec(layer, 1),
            _resident(w_all, j),
        ],
        out_specs=pl.BlockSpec((tm, n), lambda i: (i, 0)),
        out_shape=jax.ShapeDtypeStruct((m, n), F32),
        compiler_params=_cp(("parallel",)),
        name="mod_matmul",
    )(x, g, mod.arr, mod.arr, w_all)


def _resmm_kernel(a_ref, w_ref, x_ref, gate_ref, o_ref):
    o_ref[...] = x_ref[...] + gate_ref[...] * _dot(a_ref[...], w_ref[...])


def _res_matmul(a, w_all, j, x, mod, layer, gate_col, tm):
    m, k = a.shape
    n = w_all.shape[2]
    return pl.pallas_call(
        _resmm_kernel,
        grid=(m // tm,),
        in_specs=[
            pl.BlockSpec((tm, k), lambda i: (i, 0)),
            _resident(w_all, j),
            pl.BlockSpec((tm, n), lambda i: (i, 0)),
            mod.spec(layer, gate_col),
        ],
        out_specs=pl.BlockSpec((tm, n), lambda i: (i, 0)),
        out_shape=jax.ShapeDtypeStruct((m, n), F32),
        compiler_params=_cp(("parallel",)),
        name="res_matmul",
    )(a, w_all, x, mod.arr)


def _ffn_kernel(x_ref, g_ref, sh_ref, sc_ref, gate_ref, wgu_ref, wd_ref, gf_ref, o_ref, a_scr,
                *, final_norm, dff, fc):
    x = x_ref[...]
    h = (_rmsnorm(x, g_ref[...]) * (1.0 + sc_ref[...]) + sh_ref[...]).astype(BF16)
    for c in range(dff // fc):
        gg = _dot(h, wgu_ref[:, c * fc:(c + 1) * fc])
        uu = _dot(h, wgu_ref[:, dff + c * fc:dff + (c + 1) * fc])
        a_scr[:, c * fc:(c + 1) * fc] = (_silu(gg) * uu).astype(BF16)
    y = x + gate_ref[...] * _dot(a_scr[...], wd_ref[...])
    if final_norm:
        y = _rmsnorm(y, gf_ref[...])
    o_ref[...] = y


def _ffn(x, g, mod, layer, w_gu, w_down, g_final, final_norm, tm, fc):
    m, d = x.shape
    dff = w_down.shape[1]
    return pl.pallas_call(
        functools.partial(_ffn_kernel, final_norm=final_norm, dff=dff, fc=fc),
        grid=(m // tm,),
        in_specs=[
            pl.BlockSpec((tm, d), lambda i: (i, 0)),
            pl.BlockSpec((1, d), lambda i: (0, 0)),
            mod.spec(layer, 3),
            mod.spec(layer, 4),
            mod.spec(layer, 5),
            _resident(w_gu, layer),
            _resident(w_down, layer),
            pl.BlockSpec((1, d), lambda i: (0, 0)),
        ],
        out_specs=pl.BlockSpec((tm, d), lambda i: (i, 0)),
        out_shape=jax.ShapeDtypeStruct((m, d), F32),
        scratch_shapes=[pltpu.VMEM((tm, dff), BF16)],
        compiler_params=_cp(("parallel",)),
        name="ffn",
    )(x, g, mod.arr, mod.arr, mod.arr, w_gu, w_down, g_final)


def _hg_lower_bound(lbl_ref, j):
    logits = lbl_ref[...]
    e = jnp.exp(logits - jnp.max(logits, axis=0, keepdims=True))
    den = jnp.sum(e, axis=0, keepdims=True)
    lb = jnp.zeros_like(den)
    for i in range(j):
        lb = lb + e[i:i + 1, :] / den
    return lb


def _cumsum_rows(x):
    rows = x.shape[0]
    idx = lax.broadcasted_iota(jnp.int32, x.shape, 0)
    d = 1
    while d < rows:
        x = x + jnp.where(idx >= d, pltpu.roll(x, d, 0), 0.0)
        d *= 2
    return x


def _hgrn_gates(qp, fp, lb):
    logf = jnp.log(lb + (1.0 - lb) * jax.nn.sigmoid(fp))
    kk = (1.0 - lb) * jax.nn.sigmoid(-fp)
    q = _silu(qp) * (HG_DK ** -0.5)
    return q, kk, logf


def _prompt_mixer_kernel(z_ref, cw_ref, lbl_ref, gn_ref, ab_ref, conv_ref, s_ref, ubuf, st_scr,
                         *, layer_j, tt):
    t = pl.program_id(1)
    nt = pl.num_programs(1)

    @pl.when(t == 0)
    def _():
        ubuf[0:8, :] = jnp.zeros((8, CONV_CH), F32)
        st_scr[...] = jnp.zeros_like(st_scr)

    u = z_ref[:, Z_C:Z_C + CONV_CH] * z_ref[:, Z_X:Z_X + CONV_CH]
    ubuf[8:8 + tt, :] = u
    y = (cw_ref[0:1, :] * ubuf[6:6 + tt, :] + cw_ref[1:2, :] * ubuf[7:7 + tt, :]
         + cw_ref[2:3, :] * ubuf[8:8 + tt, :])
    ab_ref[:, 0:CONV_CH] = (z_ref[:, Z_B:Z_B + CONV_CH] * y).astype(BF16)
    last2 = ubuf[tt + 6:tt + 8, :]
    ubuf[6:8, :] = last2
    conv_ref[...] = last2

    lb_all = _hg_lower_bound(lbl_ref, layer_j)
    gn = gn_ref[...]
    ell = HG_SUB
    row8 = lax.broadcasted_iota(jnp.int32, (SUBLANES, HG_DK), 0)

    def chunk(c, carry):
        r0 = pl.multiple_of(c * ell, ell)
        for h in range(HG_HEADS):
            lo = h * HG_DK
            lb = lb_all[:, lo:lo + HG_DK]
            qp = z_ref[pl.ds(r0, ell), Z_Q + lo:Z_Q + lo + HG_DK]
            fp = z_ref[pl.ds(r0, ell), Z_F + lo:Z_F + lo + HG_DK]
            v = z_ref[pl.ds(r0, ell), Z_I + lo:Z_I + lo + HG_DK]
            gp = z_ref[pl.ds(r0, ell), Z_G + lo:Z_G + lo + HG_DK]
            q, kk, logf = _hgrn_gates(qp, fp, lb)
            gc = _cumsum_rows(logf)
            gl = gc[ell - 1:ell, :]
            st = st_scr[h]
            o = _dot_nt((q * jnp.exp(gc)).astype(BF16), st.astype(BF16))
            o_grp = [o[r:r + SUBLANES, :] for r in range(0, ell, SUBLANES)]
            gc2 = gc * LOG2E
            for s in range(ell):
                for gi, r in enumerate(range(0, ell, SUBLANES)):
                    if r + SUBLANES <= s:
                        continue
                    d = gc2[r:r + SUBLANES, :] - gc2[s:s + 1, :]
                    if r <= s:
                        d = jnp.where(row8 >= s - r, d, NEG)
                    w = q[r:r + SUBLANES, :] * kk[s:s + 1, :] * jnp.exp2(d)
                    o_grp[gi] = o_grp[gi] + jnp.sum(w, axis=-1, keepdims=True) * v[s:s + 1, :]
            o = jnp.concatenate(o_grp, axis=0)
            kd = kk * jnp.exp(gl - gc)
            st_scr[h] = st * jnp.exp(gl) + _dot_tn(v.astype(BF16), kd.astype(BF16))
            b = _rmsnorm(o, gn) * _silu(gp)
            ab_ref[pl.ds(r0, ell), CONV_CH + lo:CONV_CH + lo + HG_DK] = b.astype(BF16)
        return carry

    lax.fori_loop(0, tt // ell, chunk, 0, unroll=2)

    @pl.when(t == nt - 1)
    def _():
        for h in range(HG_HEADS):
            s_ref[h] = st_scr[h].T


def _prompt_mixer(z, conv_w_j, lb_logits, gnorm_j, layer_j, nb, seq, tt):
    m = z.shape[0]
    nt = seq // tt
    return pl.pallas_call(
        functools.partial(_prompt_mixer_kernel, layer_j=layer_j, tt=tt),
        grid=(nb, nt),
        in_specs=[
            pl.BlockSpec((tt, EVEN_IN), lambda n, t: (n * nt + t, 0)),
            pl.BlockSpec((CONV_W, CONV_CH), lambda n, t: (0, 0)),
            pl.BlockSpec(lb_logits.shape, lambda n, t: (0, 0)),
            pl.BlockSpec((1, HG_DK), lambda n, t: (0, 0)),
        ],
        out_specs=[
            pl.BlockSpec((tt, D_MODEL), lambda n, t: (n * nt + t, 0)),
            pl.BlockSpec((None, CONV_W - 1, CONV_CH), lambda n, t: (n, 0, 0)),
            pl.BlockSpec((None, HG_HEADS, HG_DK, HG_DK), lambda n, t: (n, 0, 0, 0)),
        ],
        out_shape=[
            jax.ShapeDtypeStruct((m, D_MODEL), BF16),
            jax.ShapeDtypeStruct((nb, CONV_W - 1, CONV_CH), F32),
            jax.ShapeDtypeStruct((nb, HG_HEADS, HG_DK, HG_DK), F32),
        ],
        scratch_shapes=[pltpu.VMEM((tt + 8, CONV_CH), F32),
                        pltpu.VMEM((HG_HEADS, HG_DK, HG_DK), F32)],
        compiler_params=_cp(("parallel", "arbitrary")),
        name="prompt_conv_hgrn",
    )(z, conv_w_j, lb_logits, gnorm_j)


def _column(row_vec, eye):
    return jnp.sum(jnp.where(eye, row_vec, 0.0), axis=1, keepdims=True)


def _sample_mixer_kernel(z_ref, cb_ref, s0_ref, cw_ref, lbl_ref, gn_ref, ab_ref, conv_ref, s_ref,
                         b_scr, *, layer_j, tb):
    u = z_ref[:, Z_C:Z_C + CONV_CH] * z_ref[:, Z_X:Z_X + CONV_CH]
    b0 = cb_ref[:, 0, :]
    b1 = cb_ref[:, 1, :]
    y = cw_ref[0:1, :] * b0 + cw_ref[1:2, :] * b1 + cw_ref[2:3, :] * u
    ab_ref[:, 0:CONV_CH] = (z_ref[:, Z_B:Z_B + CONV_CH] * y).astype(BF16)
    conv_ref[:, 0, :] = b1
    conv_ref[:, 1, :] = u

    lb_all = _hg_lower_bound(lbl_ref, layer_j)
    gn = gn_ref[...]
    eye = (lax.broadcasted_iota(jnp.int32, (HG_DK, HG_DK), 0)
           == lax.broadcasted_iota(jnp.int32, (HG_DK, HG_DK), 1))

    q_all, kk_all, logf_all = _hgrn_gates(z_ref[:, Z_Q:Z_Q + HG_WIDTH], z_ref[:, Z_F:Z_F + HG_WIDTH],
                                          lb_all)
    ef_all = jnp.exp(logf_all)
    v_all = z_ref[:, Z_I:Z_I + HG_WIDTH]
    for n in range(tb):
        for h in range(HG_HEADS):
            lo = h * HG_DK
            row = lambda a: a[n:n + 1, lo:lo + HG_DK]
            s_new = (_column(row(ef_all), eye) * s0_ref[n, h]
                     + _column(row(kk_all), eye) * row(v_all))
            s_ref[n, h] = s_new
            b_scr[n:n + 1, lo:lo + HG_DK] = jnp.sum(_column(row(q_all), eye) * s_new, axis=0,
                                                    keepdims=True)
    for h in range(HG_HEADS):
        lo = h * HG_DK
        b = _rmsnorm(b_scr[:, lo:lo + HG_DK], gn) * _silu(z_ref[:, Z_G + lo:Z_G + lo + HG_DK])
        ab_ref[:, CONV_CH + lo:CONV_CH + lo + HG_DK] = b.astype(BF16)


def _sample_mixer(z, conv_buf, s0, conv_w_j, lb_logits, gnorm_j, layer_j, tb):
    nb = z.shape[0]
    return pl.pallas_call(
        functools.partial(_sample_mixer_kernel, layer_j=layer_j, tb=tb),
        grid=(nb // tb,),
        in_specs=[
            pl.BlockSpec((tb, EVEN_IN), lambda i: (i, 0)),
            pl.BlockSpec((None, tb, CONV_W - 1, CONV_CH), lambda i: (layer_j, i, 0, 0)),
            pl.BlockSpec((None, tb, HG_HEADS, HG_DK, HG_DK), lambda i: (layer_j, i, 0, 0, 0)),
            pl.BlockSpec((CONV_W, CONV_CH), lambda i: (0, 0)),
            pl.BlockSpec(lb_logits.shape, lambda i: (0, 0)),
            pl.BlockSpec((1, HG_DK), lambda i: (0, 0)),
        ],
        out_specs=[
            pl.BlockSpec((tb, D_MODEL), lambda i: (i, 0)),
            pl.BlockSpec((tb, CONV_W - 1, CONV_CH), lambda i: (i, 0, 0)),
            pl.BlockSpec((tb, HG_HEADS, HG_DK, HG_DK), lambda i: (i, 0, 0, 0)),
        ],
        out_shape=[
            jax.ShapeDtypeStruct((nb, D_MODEL), BF16),
            jax.ShapeDtypeStruct((nb, CONV_W - 1, CONV_CH), F32),
            jax.ShapeDtypeStruct((nb, HG_HEADS, HG_DK, HG_DK), F32),
        ],
        scratch_shapes=[pltpu.VMEM((tb, HG_WIDTH), F32)],
        compiler_params=_cp(("parallel",)),
        name="sample_conv_hgrn",
    )(z, conv_buf, s0, conv_w_j, lb_logits, gnorm_j)


def _rope_table_kernel(cos_ref, sin_ref, *, tr, pos0, step):
    lane = lax.broadcasted_iota(jnp.int32, (tr, LANES), 1)
    rowi = lax.broadcasted_iota(jnp.int32, (tr, LANES), 0)
    half = ROPE_D // 2
    fi = (lane % half).astype(F32)
    inv = jnp.exp(fi * (-math.log(ROPE_THETA) / half))
    pos = (pos0 + step * (pl.program_id(0) * tr + rowi)).astype(F32)
    ang = pos * inv
    sign = jnp.where((lane % ROPE_D) < half, -1.0, 1.0)
    cos_ref[...] = jnp.cos(ang)
    sin_ref[...] = jnp.sin(ang) * sign


def _rope_table(rows, pos0, step, tr):
    return pl.pallas_call(
        functools.partial(_rope_table_kernel, tr=tr, pos0=pos0, step=step),
        grid=(rows // tr,),
        out_specs=[pl.BlockSpec((tr, LANES), lambda i: (i, 0))] * 2,
        out_shape=[jax.ShapeDtypeStruct((rows, LANES), F32)] * 2,
        compiler_params=_cp(("parallel",)),
        name="rope_table",
    )()


def _rope_pairs(g, cos, sin_signed):
    lane = lax.broadcasted_iota(jnp.int32, g.shape, 1)
    half = ROPE_D // 2
    n = g.shape[1]
    rot = jnp.where((lane % ROPE_D) < half, pltpu.roll(g, n - half, 1), pltpu.roll(g, half, 1))
    return g * cos + rot * sin_signed


def _dqkv_kernel(x_ref, g_ref, sh_ref, sc_ref, w_ref, qn_ref, kvn_ref, cos_ref, sin_ref,
                 cq_ref, ckv_ref, kr_ref, ckvb_ref, krb_ref):
    h = _rmsnorm(x_ref[...], g_ref[...]) * (1.0 + sc_ref[...]) + sh_ref[...]
    d = _dot(h.astype(BF16), w_ref[...])
    cq_ref[...] = _rmsnorm(d[:, :Q_LORA], qn_ref[...]).astype(BF16)
    ckv = _rmsnorm(d[:, Q_LORA:Q_LORA + KV_LORA], kvn_ref[...])
    ckv_ref[...] = ckv
    ckvb_ref[...] = ckv.astype(BF16)
    kr = _rope_pairs(d[:, Q_LORA + KV_LORA:], cos_ref[...], sin_ref[...])[:, :ROPE_D]
    kr_ref[...] = kr
    krb_ref[...] = kr.astype(BF16)


def _dqkv(x, g, mod, layer, w_pad, q_norm, kv_norm, cos_t, sin_t, tm, rope_blocks):
    m, d = x.shape
    n = w_pad.shape[1]
    rb = rope_blocks
    return pl.pallas_call(
        _dqkv_kernel,
        grid=(m // tm,),
        in_specs=[
            pl.BlockSpec((tm, d), lambda i: (i, 0)),
            pl.BlockSpec((1, d), lambda i: (0, 0)),
            mod.spec(layer, 0),
            mod.spec(layer, 1),
            pl.BlockSpec((d, n), lambda i: (0, 0)),
            pl.BlockSpec((1, Q_LORA), lambda i: (0, 0)),
            pl.BlockSpec((1, KV_LORA), lambda i: (0, 0)),
            pl.BlockSpec((cos_t.shape[0] // rb, LANES), lambda i: (i % rb, 0)),
            pl.BlockSpec((cos_t.shape[0] // rb, LANES), lambda i: (i % rb, 0)),
        ],
        out_specs=[
            pl.BlockSpec((tm, Q_LORA), lambda i: (i, 0)),
            pl.BlockSpec((tm, KV_LORA), lambda i: (i, 0)),
            pl.BlockSpec((tm, ROPE_D), lambda i: (i, 0)),
            pl.BlockSpec((tm, KV_LORA), lambda i: (i, 0)),
            pl.BlockSpec((tm, ROPE_D), lambda i: (i, 0)),
        ],
        out_shape=[
            jax.ShapeDtypeStruct((m, Q_LORA), BF16),
            jax.ShapeDtypeStruct((m, KV_LORA), F32),
            jax.ShapeDtypeStruct((m, ROPE_D), F32),
            jax.ShapeDtypeStruct((m, KV_LORA), BF16),
            jax.ShapeDtypeStruct((m, ROPE_D), BF16),
        ],
        compiler_params=_cp(("parallel",)),
        name="mla_down_proj",
    )(x, g, mod.arr, mod.arr, w_pad, q_norm, kv_norm, cos_t, sin_t)


def _q_kernel(cq_ref, wn_ref, wr_ref, wuk_ref, cos_ref, sin_ref, ql_ref, qr_ref):
    cq = cq_ref[...]
    qn = _dot(cq, wn_ref[...])
    qr = _dot(cq, wr_ref[...])
    cos = jnp.concatenate([cos_ref[...]] * (MLA_HEADS * ROPE_D // LANES), axis=1)
    sin = jnp.concatenate([sin_ref[...]] * (MLA_HEADS * ROPE_D // LANES), axis=1)
    qr = _rope_pairs(qr, cos, sin).astype(BF16)
    for h in range(MLA_HEADS):
        ql_ref[h] = _dot(qn[:, h * NOPE:(h + 1) * NOPE].astype(BF16), wuk_ref[h]).astype(BF16)
        qr_ref[h] = qr[:, h * ROPE_D:(h + 1) * ROPE_D]


def _q_proj(cq, w_nope, w_rope, w_ukt, cos_t, sin_t, nb, seq, tm, rope_blocks):
    nt = seq // tm
    rb = rope_blocks
    return pl.pallas_call(
        _q_kernel,
        grid=(nb, nt),
        in_specs=[
            pl.BlockSpec((tm, Q_LORA), lambda n, t: (n * nt + t, 0)),
            pl.BlockSpec(w_nope.shape, lambda n, t: (0, 0)),
            pl.BlockSpec(w_rope.shape, lambda n, t: (0, 0)),
            pl.BlockSpec(w_ukt.shape, lambda n, t: (0, 0, 0)),
            pl.BlockSpec((cos_t.shape[0] // rb, LANES), lambda n, t: (t % rb, 0)),
            pl.BlockSpec((cos_t.shape[0] // rb, LANES), lambda n, t: (t % rb, 0)),
        ],
        out_specs=[
            pl.BlockSpec((None, MLA_HEADS, tm, KV_LORA), lambda n, t: (n, 0, t, 0)),
            pl.BlockSpec((None, MLA_HEADS, tm, ROPE_D), lambda n, t: (n, 0, t, 0)),
        ],
        out_shape=[
            jax.ShapeDtypeStruct((nb, MLA_HEADS, seq, KV_LORA), BF16),
            jax.ShapeDtypeStruct((nb, MLA_HEADS, seq, ROPE_D), BF16),
        ],
        compiler_params=_cp(("parallel", "parallel")),
        name="mla_q_proj",
    )(cq, w_nope, w_rope, w_ukt, cos_t, sin_t)


def _lane_tile(x, width):
    return x if width == LANES else jnp.concatenate([x] * (width // LANES), axis=1)


def _prompt_attn_kernel(ql_ref, qr_ref, k_ref, kr_ref, o_ref, m_scr, l_scr, acc_scr, *, tq):
    i = pl.program_id(1)
    m_scr[...] = jnp.full_like(m_scr, NEG)
    l_scr[...] = jnp.zeros_like(l_scr)
    acc_scr[...] = jnp.zeros_like(acc_scr)

    def block(j, masked):
        r0 = pl.multiple_of(j * tq, tq)
        k = k_ref[pl.ds(r0, tq), :]
        kr = kr_ref[pl.ds(r0, tq), :]
        if masked:
            causal = (lax.broadcasted_iota(jnp.int32, (tq, tq), 1)
                      <= lax.broadcasted_iota(jnp.int32, (tq, tq), 0))

        def head(h, carry):
            s = (_dot_nt(ql_ref[h], k) + _dot_nt(qr_ref[h], kr)) * (MLA_SCALE * LOG2E)
            if masked:
                s = jnp.where(causal, s, NEG)
            m_prev = m_scr[h]
            m_new = jnp.maximum(m_prev, jnp.max(s, axis=-1, keepdims=True))
            alpha = jnp.exp2(m_prev - m_new)
            p = jnp.exp2(s - _lane_tile(m_new, tq))
            l_scr[h] = alpha * l_scr[h] + jnp.sum(p, axis=-1, keepdims=True)
            acc_scr[h] = _lane_tile(alpha, KV_LORA) * acc_scr[h] + _dot(p.astype(BF16), k)
            m_scr[h] = m_new
            return carry

        lax.fori_loop(0, MLA_HEADS, head, 0, unroll=True)

    def body(j, carry):
        block(j, False)
        return carry

    lax.fori_loop(0, i, body, 0)
    block(i, True)
    for h in range(MLA_HEADS):
        ctx = acc_scr[h] / _lane_tile(l_scr[h], KV_LORA)
        o_ref[:, h * KV_LORA:(h + 1) * KV_LORA] = ctx.astype(BF16)


def _prompt_attn(q_lat, q_rope, ckv_b, kr_b, nb, seq, tq):
    nq = seq // tq
    return pl.pallas_call(
        functools.partial(_prompt_attn_kernel, tq=tq),
        grid=(nb, nq),
        in_specs=[
            pl.BlockSpec((None, MLA_HEADS, tq, KV_LORA), lambda n, i: (n, 0, i, 0)),
            pl.BlockSpec((None, MLA_HEADS, tq, ROPE_D), lambda n, i: (n, 0, i, 0)),
            pl.BlockSpec((seq, KV_LORA), lambda n, i: (n, 0)),
            pl.BlockSpec((seq, ROPE_D), lambda n, i: (n, 0)),
        ],
        out_specs=pl.BlockSpec((tq, MLA_HEADS * KV_LORA), lambda n, i: (n * nq + i, 0)),
        out_shape=jax.ShapeDtypeStruct((nb * seq, MLA_HEADS * KV_LORA), BF16),
        scratch_shapes=[pltpu.VMEM((MLA_HEADS, tq, LANES), F32),
                        pltpu.VMEM((MLA_HEADS, tq, LANES), F32),
                        pltpu.VMEM((MLA_HEADS, tq, KV_LORA), F32)],
        compiler_params=_cp(("parallel", "arbitrary")),
        name="mla_prompt_attn",
    )(q_lat, q_rope, ckv_b, kr_b)


def _sample_attn_kernel(pt_ref, ql_ref, qr_ref, cn_ref, krn_ref, ckv_hbm, krt_hbm, o_ref,
                        kbuf, rbuf, kb16, kt16, sems, *, layer_j, n_pages):
    b = pl.program_id(0)
    nb = pl.num_programs(0)
    slot = b % 2

    def page_copies(seq, slot_, p):
        page = pt_ref[seq, p]
        return (
            pltpu.make_async_copy(ckv_hbm.at[layer_j, page],
                                  kbuf.at[slot_, p * PAGE_SIZE:(p + 1) * PAGE_SIZE, :],
                                  sems.at[0, slot_]),
            pltpu.make_async_copy(krt_hbm.at[layer_j, page],
                                  rbuf.at[slot_, :, p * PAGE_SIZE:(p + 1) * PAGE_SIZE],
                                  sems.at[1, slot_]),
        )

    def start_all(seq, slot_):
        for p in range(n_pages):
            for cp in page_copies(seq, slot_, p):
                cp.start()

    @pl.when(b == 0)
    def _():
        start_all(0, 0)

    @pl.when(b + 1 < nb)
    def _():
        start_all(b + 1, 1 - slot)

    for p in range(n_pages):
        for cp in page_copies(b, slot, p):
            cp.wait()

    ql = ql_ref[...]
    qr = qr_ref[...]
    past = n_pages * PAGE_SIZE
    tc = min(past, 1024)
    for c in range(past // tc):
        kb = kbuf[slot, c * tc:(c + 1) * tc, :].astype(BF16)
        kb16[c * tc:(c + 1) * tc, :] = kb
        kt16[:, c * tc:(c + 1) * tc] = kb.T
    s = (_dot(ql, kt16[...]) + _dot(qr, rbuf[slot].astype(BF16))) * MLA_SCALE
    cn = cn_ref[...]
    s_new = (jnp.sum(ql.astype(F32) * cn, axis=-1, keepdims=True)
             + jnp.sum(qr.astype(F32) * krn_ref[...], axis=-1, keepdims=True)) * MLA_SCALE
    m = jnp.maximum(jnp.max(s, axis=-1, keepdims=True), s_new)
    p = jnp.exp(s - m)
    p_new = jnp.exp(s_new - m)
    l = jnp.sum(p, axis=-1, keepdims=True) + p_new
    acc = _dot(p.astype(BF16), kb16[...]) + p_new * cn
    o_ref[...] = (acc / l).astype(BF16)


def _sample_attn(page_table, q_lat, q_rope, ckv_new, kr_new, cache_ckv, cache_krope_t, layer_j):
    nb, n_pages = page_table.shape
    past = n_pages * PAGE_SIZE
    grid_spec = pltpu.PrefetchScalarGridSpec(
        num_scalar_prefetch=1,
        grid=(nb,),
        in_specs=[
            pl.BlockSpec((None, MLA_HEADS, KV_LORA), lambda b, pt: (b, 0, 0)),
            pl.BlockSpec((None, MLA_HEADS, ROPE_D), lambda b, pt: (b, 0, 0)),
            pl.BlockSpec((None, 1, KV_LORA), lambda b, pt: (b, 0, 0)),
            pl.BlockSpec((None, 1, ROPE_D), lambda b, pt: (b, 0, 0)),
            pl.BlockSpec(memory_space=pl.ANY),
            pl.BlockSpec(memory_space=pl.ANY),
        ],
        out_specs=pl.BlockSpec((None, MLA_HEADS, KV_LORA), lambda b, pt: (b, 0, 0)),
        scratch_shapes=[
            pltpu.VMEM((2, past, KV_LORA), F32),
            pltpu.VMEM((2, ROPE_D, past), F32),
            pltpu.VMEM((past, KV_LORA), BF16),
            pltpu.VMEM((KV_LORA, past), BF16),
            pltpu.SemaphoreType.DMA((2, 2)),
        ],
    )
    return pl.pallas_call(
        functools.partial(_sample_attn_kernel, layer_j=layer_j, n_pages=n_pages),
        grid_spec=grid_spec,
        out_shape=jax.ShapeDtypeStruct((nb, MLA_HEADS, KV_LORA), BF16),
        compiler_params=_cp(("arbitrary",)),
        name="mla_sample_attn",
    )(page_table, q_lat, q_rope, ckv_new.reshape(nb, 1, KV_LORA), kr_new.reshape(nb, 1, ROPE_D),
      cache_ckv, cache_krope_t)


def _mla_out_kernel(ctx_ref, wuv_ref, wo_ref, x_ref, gate_ref, o_ref):
    parts = [_dot(ctx_ref[:, h * KV_LORA:(h + 1) * KV_LORA], wuv_ref[h]).astype(BF16)
             for h in range(MLA_HEADS)]
    o = jnp.concatenate(parts, axis=1)
    o_ref[...] = x_ref[...] + gate_ref[...] * _dot(o, wo_ref[...])


def _mla_out(ctx, w_uv_h, w_o, x, mod, layer, tm):
    m, d = x.shape
    return pl.pallas_call(
        _mla_out_kernel,
        grid=(m // tm,),
        in_specs=[
            pl.BlockSpec((tm, MLA_HEADS * KV_LORA), lambda i: (i, 0)),
            pl.BlockSpec(w_uv_h.shape, lambda i: (0, 0, 0)),
            pl.BlockSpec(w_o.shape, lambda i: (0, 0)),
            pl.BlockSpec((tm, d), lambda i: (i, 0)),
            mod.spec(layer, 2),
        ],
        out_specs=pl.BlockSpec((tm, d), lambda i: (i, 0)),
        out_shape=jax.ShapeDtypeStruct((m, d), F32),
        compiler_params=_cp(("parallel",)),
        name="mla_out_proj",
    )(ctx, w_uv_h, w_o, x, mod.arr)


def _tile(m, pref):
    t = min(m, pref)
    assert m % t == 0, (m, t)
    return t


def kernel(x_prompt, x_sample, c_prompt, c_sample, cache_conv, state_hgrn, cache_ckv, cache_krope,
           page_table, w_ada, b_ada, norm_mix, norm_ffn, w_in_even, conv_w, hg_lb_logits, hg_gnorm,
           w_out_even, w_dqkv, q_norm, w_uq, kv_norm, w_uk, w_uv, w_o, w_gu, w_down, norm_final):
    bp, tp, d = x_prompt.shape
    bs, ts, _ = x_sample.shape
    assert ts == 1 and d == D_MODEL
    depth = w_ada.shape[0]
    n_pages = page_table.shape[1]
    past_len = n_pages * PAGE_SIZE
    mp, ms = bp * tp, bs

    tm_p = _tile(tp, 512)
    tm_s = ms
    tt = _tile(tp, 256)
    tq = _tile(tp, 512)
    tb = _tile(bs, 16)
    krope_t = jnp.swapaxes(cache_krope, 2, 3)
    fc = 256

    mods = _ada_all(jnp.concatenate([c_prompt, c_sample], axis=0), w_ada, b_ada)
    mod_p = _Mod(mods[:, :bp].reshape(depth, bp, 1, N_MOD * d), tp // tm_p)
    mod_s = _Mod(mods[:, bp:].reshape(depth, 1, bs, N_MOD * d), 1)

    cos_p, sin_p = _rope_table(tp, 0, 1, _tile(tp, 512))
    cos_s, sin_s = _rope_table(8, past_len, 0, 8)
    cos_s = jnp.broadcast_to(cos_s[:1], (ms, LANES))
    sin_s = jnp.broadcast_to(sin_s[:1], (ms, LANES))

    w_in_b = w_in_even.astype(BF16)
    w_out_b = w_out_even.astype(BF16)
    w_gu_b = w_gu.astype(BF16)
    w_down_b = w_down.astype(BF16)

    xp = x_prompt.reshape(mp, d)
    xs = x_sample.reshape(ms, d)
    outs ={k: [] for k in ("conv_p", "hg_p", "ckv_p", "kr_p", "conv_s", "hg_s", "ckv_s", "kr_s")}

    for l in range(depth):
        j = l // 2
        g_mix = norm_mix[l].reshape(1, d)
        g_ffn = norm_ffn[l].reshape(1, d)
        if l % 2 == 0:
            gn = hg_gnorm[j].reshape(1, HG_DK)
            zp = _mod_matmul(xp, g_mix, mod_p, l, w_in_b, j, tm_p)
            zs = _mod_matmul(xs, g_mix, mod_s, l, w_in_b, j, tm_s)
            abp, cvp, hgp = _prompt_mixer(zp, conv_w[j], hg_lb_logits, gn, j, bp, tp, tt)
            abs_, cvs, hgs = _sample_mixer(zs, cache_conv, state_hgrn, conv_w[j], hg_lb_logits,
                                           gn, j, tb)
            xp = _res_matmul(abp, w_out_b, j, xp, mod_p, l, 2, tm_p)
            xs = _res_matmul(abs_, w_out_b, j, xs, mod_s, l, 2, tm_s)
            outs["conv_p"].append(cvp)
            outs["hg_p"].append(hgp)
            outs["conv_s"].append(cvs)
            outs["hg_s"].append(hgs)
        else:
            w_d = jnp.pad(w_dqkv[j], ((0, 0), (0, ROPE_D))).astype(BF16)
            wq = w_uq[j].reshape(Q_LORA, MLA_HEADS, NOPE + ROPE_D)
            w_nope = wq[:, :, :NOPE].reshape(Q_LORA, MLA_HEADS * NOPE).astype(BF16)
            w_rope = wq[:, :, NOPE:].reshape(Q_LORA, MLA_HEADS * ROPE_D).astype(BF16)
            w_ukt = jnp.transpose(w_uk[j], (1, 2, 0)).astype(BF16)
            w_uvh = jnp.transpose(w_uv[j], (1, 0, 2)).astype(BF16)
            w_oj = w_o[j].astype(BF16)
            qn = q_norm[j].reshape(1, Q_LORA)
            kvn = kv_norm[j].reshape(1, KV_LORA)

            cqp, ckvp, krp, ckvpb, krpb = _dqkv(xp, g_mix, mod_p, l, w_d, qn, kvn, cos_p, sin_p,
                                                tm_p, tp // tm_p)
            cqs, ckvs, krs, _, _ = _dqkv(xs, g_mix, mod_s, l, w_d, qn, kvn, cos_s, sin_s, tm_s, 1)
            qlp, qrp = _q_proj(cqp, w_nope, w_rope, w_ukt, cos_p, sin_p, bp, tp, tm_p, tp // tm_p)
            qls, qrs = _q_proj(cqs, w_nope, w_rope, w_ukt, cos_s, sin_s, 1, ms, tm_s, 1)
            ctxp = _prompt_attn(qlp, qrp, ckvpb, krpb, bp, tp, tq)
            ctxs = _sample_attn(page_table,
                                jnp.transpose(qls[0], (1, 0, 2)), jnp.transpose(qrs[0], (1, 0, 2)),
                                ckvs, krs, cache_ckv, krope_t, j)
            xp = _mla_out(ctxp, w_uvh, w_oj, xp, mod_p, l, tm_p)
            xs = _mla_out(ctxs.reshape(ms, MLA_HEADS * KV_LORA), w_uvh, w_oj, xs, mod_s, l, tm_s)
            outs["ckv_p"].append(ckvp.reshape(bp, tp, KV_LORA))
            outs["kr_p"].append(krp.reshape(bp, tp, ROPE_D))
            outs["ckv_s"].append(ckvs.reshape(bs, ts, KV_LORA))
            outs["kr_s"].append(krs.reshape(bs, ts, ROPE_D))

        last = l == depth - 1
        gfin = norm_final.reshape(1, d)
        xp = _ffn(xp, g_ffn, mod_p, l, w_gu_b, w_down_b, gfin, last, tm_p, fc)
        xs = _ffn(xs, g_ffn, mod_s, l, w_gu_b, w_down_b, gfin, last, tm_s, fc)

    return (xp.reshape(bp, tp, d), xs.reshape(bs, ts, d),
            jnp.stack(outs["conv_p"]), jnp.stack(outs["hg_p"]),
            jnp.stack(outs["ckv_p"]), jnp.stack(outs["kr_p"]),
            jnp.stack(outs["conv_s"]), jnp.stack(outs["hg_s"]),
            jnp.stack(outs["ckv_s"]), jnp.stack(outs["kr_s"]))
```

```python
import functools
import math

import jax
import jax.numpy as jnp
from jax import lax
from jax.experimental import pallas as pl
from jax.experimental.pallas import tpu as pltpu

F32 = jnp.float32
BF16 = jnp.bfloat16

D_MODEL = 1024
N_MOD = 6
EPS = 1e-6
NEG = -1e30
PAGE_SIZE = 128
CONV_CH = D_MODEL // 2
CONV_W = 3
HG_WIDTH = D_MODEL // 2
HG_DK = 128
HG_HEADS = HG_WIDTH // HG_DK
EVEN_IN = 3 * CONV_CH + 4 * HG_WIDTH
MLA_HEADS = 8
NOPE = 128
ROPE_D = 64
V_D = 128
Q_LORA = 384
KV_LORA = 256
ROPE_THETA = 10000.0
MLA_SCALE = (NOPE + ROPE_D) ** -0.5
LANES = 128
SUBLANES = 8
LOG2E = 1.0 / math.log(2.0)

Z_B, Z_C, Z_X = 0, CONV_CH, 2 * CONV_CH
Z_Q = 3 * CONV_CH
Z_F = Z_Q + HG_WIDTH
Z_I = Z_F + HG_WIDTH
Z_G = Z_I + HG_WIDTH

HG_SUB = 16
HG_UNROLL = 2
VMEM_LIMIT = 56 * 1024 * 1024


def _cp(sem, vmem=VMEM_LIMIT):
    return pltpu.CompilerParams(dimension_semantics=sem, vmem_limit_bytes=vmem)


def _silu(x):
    return x * jax.nn.sigmoid(x)


def _rmsnorm(x, g):
    return x * lax.rsqrt(jnp.mean(x * x, axis=-1, keepdims=True) + EPS) * g


def _dot(a, b):
    return jnp.dot(a, b, preferred_element_type=F32)


def _dot_nt(a, b):
    return lax.dot_general(a, b, (((1,), (1,)), ((), ())), preferred_element_type=F32)


def _dot_tn(a, b):
    return lax.dot_general(a, b, (((0,), (0,)), ((), ())), preferred_element_type=F32)


def _ada_kernel(c_ref, w_ref, b_ref, o_ref):
    a = _silu(c_ref[...]).astype(BF16)
    o_ref[...] = _dot(a, w_ref[...].astype(BF16)) + b_ref[...]


def _ada_all(c_all, w_ada, b_ada, tn=1536):
    depth, d, n6 = w_ada.shape
    rows = c_all.shape[0]
    return pl.pallas_call(
        _ada_kernel,
        grid=(depth, n6 // tn),
        in_specs=[
            pl.BlockSpec((rows, d), lambda l, j: (0, 0)),
            pl.BlockSpec((None, d, tn), lambda l, j: (l, 0, j)),
            pl.BlockSpec((None, 1, tn), lambda l, j: (l, 0, j)),
        ],
        out_specs=pl.BlockSpec((None, rows, tn), lambda l, j: (l, 0, j)),
        out_shape=jax.ShapeDtypeStruct((depth, rows, n6), F32),
        compiler_params=_cp(("parallel", "parallel")),
        name="adaln_mod",
    )(c_all, w_ada, b_ada.reshape(depth, 1, n6))


class _Mod:
    def __init__(self, arr, tps):
        self.arr = arr
        self.tps = tps
        self.r = arr.shape[2]

    def spec(self, layer, col, width=D_MODEL, ncol=None):
        tps = self.tps
        per = D_MODEL // width
        if ncol is None:
            return pl.BlockSpec((None, None, self.r, width),
                                lambda i, *_: (layer, i // tps, 0, col * per))
        return pl.BlockSpec((None, None, self.r, width),
                            lambda i, j, *_: (layer, i // tps, 0, col * per + j))


def _resident(stacked, layer):
    shape = stacked.shape[1:]
    return pl.BlockSpec((None,) + shape, lambda *_: (layer,) + (0,) * len(shape),
                        pipeline_mode=pl.Buffered(1))


def _modmm_kernel(x_ref, g_ref, sh_ref, sc_ref, w_ref, o_ref):
    h = _rmsnorm(x_ref[...], g_ref[...]) * (1.0 + sc_ref[...]) + sh_ref[...]
    o_ref[...] = _dot(h.astype(BF16), w_ref[...])


def _mod_matmul(x, g, mod, layer, w_all, j, tm):
    m, d = x.shape
    n = w_all.shape[2]
    return pl.pallas_call(
        _modmm_kernel,
        grid=(m // tm,),
        in_specs=[
            pl.BlockSpec((tm, d), lambda i: (i, 0)),
            pl.BlockSpec((1, d), lambda i: (0, 0)),
            mod.spec(layer, 0),
            mod.spec(layer, 1),
            _resident(w_all, j),
        ],
        out_specs=pl.BlockSpec((tm, n), lambda i: (i, 0)),
        out_shape=jax.ShapeDtypeStruct((m, n), F32),
        compiler_params=_cp(("parallel",)),
        name="mod_matmul",
    )(x, g, mod.arr, mod.arr, w_all)


def _ffn_kernel(*refs, final_norm, dff, fc, n_mix):
    mix = refs[:n_mix]
    (x_ref, gmix_ref, g_ref, sh_ref, sc_ref, gate_ref, wgu_ref, wd_ref, gf_ref, o_ref,
     a_scr) = refs[n_mix:]
    if n_mix == 2:
        a_ref, wout_ref = mix
        y_mix = _dot(a_ref[...], wout_ref[...])
    else:
        ctx_ref, wuv_ref, wo_ref = mix
        parts = [_dot(ctx_ref[:, h * KV_LORA:(h + 1) * KV_LORA], wuv_ref[h]).astype(BF16)
                 for h in range(MLA_HEADS)]
        y_mix = _dot(jnp.concatenate(parts, axis=1), wo_ref[...])
    x = x_ref[...] + gmix_ref[...] * y_mix
    h = (_rmsnorm(x, g_ref[...]) * (1.0 + sc_ref[...]) + sh_ref[...]).astype(BF16)
    for c in range(dff // fc):
        gg = _dot(h, wgu_ref[:, c * fc:(c + 1) * fc])
        uu = _dot(h, wgu_ref[:, dff + c * fc:dff + (c + 1) * fc])
        a_scr[:, c * fc:(c + 1) * fc] = (_silu(gg) * uu).astype(BF16)
    y = x + gate_ref[...] * _dot(a_scr[...], wd_ref[...])
    if final_norm:
        y = _rmsnorm(y, gf_ref[...])
    o_ref[...] = y


def _ffn(mix, mix_specs, x, g, mod, layer, w_gu, w_down, g_final, final_norm, tm, fc):
    m, d = x.shape
    dff = w_down.shape[1]
    return pl.pallas_call(
        functools.partial(_ffn_kernel, final_norm=final_norm, dff=dff, fc=fc, n_mix=len(mix)),
        grid=(m // tm,),
        in_specs=list(mix_specs) + [
            pl.BlockSpec((tm, d), lambda i: (i, 0)),
            mod.spec(layer, 2),
            pl.BlockSpec((1, d), lambda i: (0, 0)),
            mod.spec(layer, 3),
            mod.spec(layer, 4),
            mod.spec(layer, 5),
            _resident(w_gu, layer),
            _resident(w_down, layer),
            pl.BlockSpec((1, d), lambda i: (0, 0)),
        ],
        out_specs=pl.BlockSpec((tm, d), lambda i: (i, 0)),
        out_shape=jax.ShapeDtypeStruct((m, d), F32),
        scratch_shapes=[pltpu.VMEM((tm, dff), BF16)],
        compiler_params=_cp(("parallel",)),
        name="mix_out_ffn",
    )(*mix, x, mod.arr, g, mod.arr, mod.arr, mod.arr, w_gu, w_down, g_final)


def _hg_lower_bound(lbl_ref, j):
    logits = lbl_ref[...]
    e = jnp.exp(logits - jnp.max(logits, axis=0, keepdims=True))
    den = jnp.sum(e, axis=0, keepdims=True)
    lb = jnp.zeros_like(den)
    for i in range(j):
        lb = lb + e[i:i + 1, :] / den
    return lb


def _cumsum_rows(x):
    rows = x.shape[0]
    idx = lax.broadcasted_iota(jnp.int32, x.shape, 0)
    d = 1
    while d < rows:
        x = x + jnp.where(idx >= d, pltpu.roll(x, d, 0), 0.0)
        d *= 2
    return x


def _hgrn_gates(qp, fp, lb):
    logf = jnp.log(lb + (1.0 - lb) * jax.nn.sigmoid(fp))
    kk = (1.0 - lb) * jax.nn.sigmoid(-fp)
    q = _silu(qp) * (HG_DK ** -0.5)
    return q, kk, logf


def _prompt_mixer_kernel(z_ref, cw_ref, lbl_ref, gn_ref, ab_ref, conv_ref, s_ref, ubuf, st_scr,
                         *, layer_j, tt, ns):
    t = pl.program_id(1)
    nt = pl.num_programs(1)

    @pl.when(t == 0)
    def _():
        ubuf[:, 0:8, :] = jnp.zeros((ns, 8, CONV_CH), F32)
        st_scr[...] = jnp.zeros_like(st_scr)

    for n in range(ns):
        u = z_ref[n, :, Z_C:Z_C + CONV_CH] * z_ref[n, :, Z_X:Z_X + CONV_CH]
        ubuf[n, 8:8 + tt, :] = u
        y = (cw_ref[0:1, :] * ubuf[n, 6:6 + tt, :] + cw_ref[1:2, :] * ubuf[n, 7:7 + tt, :]
             + cw_ref[2:3, :] * ubuf[n, 8:8 + tt, :])
        ab_ref[n, :, 0:CONV_CH] = (z_ref[n, :, Z_B:Z_B + CONV_CH] * y).astype(BF16)
        last2 = ubuf[n, tt + 6:tt + 8, :]
        ubuf[n, 6:8, :] = last2
        conv_ref[n] = last2

    lb_all = _hg_lower_bound(lbl_ref, layer_j)
    gn = gn_ref[...]
    ell = HG_SUB
    row8 = lax.broadcasted_iota(jnp.int32, (SUBLANES, HG_DK), 0)

    def chunk(c, carry):
        r0 = pl.multiple_of(c * ell, ell)
        for n in range(ns):
            for h in range(HG_HEADS):
                lo = h * HG_DK
                lb = lb_all[:, lo:lo + HG_DK]
                qp = z_ref[n, pl.ds(r0, ell), Z_Q + lo:Z_Q + lo + HG_DK]
                fp = z_ref[n, pl.ds(r0, ell), Z_F + lo:Z_F + lo + HG_DK]
                v = z_ref[n, pl.ds(r0, ell), Z_I + lo:Z_I + lo + HG_DK]
                gp = z_ref[n, pl.ds(r0, ell), Z_G + lo:Z_G + lo + HG_DK]
                q, kk, logf = _hgrn_gates(qp, fp, lb)
                gc = _cumsum_rows(logf)
                gl = gc[ell - 1:ell, :]
                st = st_scr[n, h]
                o = _dot_nt((q * jnp.exp(gc)).astype(BF16), st.astype(BF16))
                o_grp = [o[r:r + SUBLANES, :] for r in range(0, ell, SUBLANES)]
                gc2 = gc * LOG2E
                for s in range(ell):
                    for gi, r in enumerate(range(0, ell, SUBLANES)):
                        if r + SUBLANES <= s:
                            continue
                        d = gc2[r:r + SUBLANES, :] - gc2[s:s + 1, :]
                        if r <= s:
                            d = jnp.where(row8 >= s - r, d, NEG)
                        w = q[r:r + SUBLANES, :] * kk[s:s + 1, :] * jnp.exp2(d)
                        o_grp[gi] = (o_grp[gi]
                                     + jnp.sum(w, axis=-1, keepdims=True) * v[s:s + 1, :])
                o = jnp.concatenate(o_grp, axis=0)
                kd = kk * jnp.exp(gl - gc)
                st_scr[n, h] = st * jnp.exp(gl) + _dot_tn(v.astype(BF16), kd.astype(BF16))
                b = _rmsnorm(o, gn) * _silu(gp)
                ab_ref[n, pl.ds(r0, ell), CONV_CH + lo:CONV_CH + lo + HG_DK] = b.astype(BF16)
        return carry

    lax.fori_loop(0, tt // ell, chunk, 0, unroll=HG_UNROLL)

    @pl.when(t == nt - 1)
    def _():
        for n in range(ns):
            for h in range(HG_HEADS):
                s_ref[n, h] = st_scr[n, h].T


def _prompt_mixer(z, conv_w_j, lb_logits, gnorm_j, layer_j, nb, seq, tt, ns):
    nt = seq // tt
    return pl.pallas_call(
        functools.partial(_prompt_mixer_kernel, layer_j=layer_j, tt=tt, ns=ns),
        grid=(nb // ns, nt),
        in_specs=[
            pl.BlockSpec((ns, tt, EVEN_IN), lambda g, t: (g, t, 0)),
            pl.BlockSpec((CONV_W, CONV_CH), lambda g, t: (0, 0)),
            pl.BlockSpec(lb_logits.shape, lambda g, t: (0, 0)),
            pl.BlockSpec((1, HG_DK), lambda g, t: (0, 0)),
        ],
        out_specs=[
            pl.BlockSpec((ns, tt, D_MODEL), lambda g, t: (g, t, 0)),
            pl.BlockSpec((ns, CONV_W - 1, CONV_CH), lambda g, t: (g, 0, 0)),
            pl.BlockSpec((ns, HG_HEADS, HG_DK, HG_DK), lambda g, t: (g, 0, 0, 0)),
        ],
        out_shape=[
            jax.ShapeDtypeStruct((nb, seq, D_MODEL), BF16),
            jax.ShapeDtypeStruct((nb, CONV_W - 1, CONV_CH), F32),
            jax.ShapeDtypeStruct((nb, HG_HEADS, HG_DK, HG_DK), F32),
        ],
        scratch_shapes=[pltpu.VMEM((ns, tt + 8, CONV_CH), F32),
                        pltpu.VMEM((ns, HG_HEADS, HG_DK, HG_DK), F32)],
        compiler_params=_cp(("parallel", "arbitrary")),
        name="prompt_conv_hgrn",
    )(z, conv_w_j, lb_logits, gnorm_j)


def _column(row_vec, eye):
    return jnp.sum(jnp.where(eye, row_vec, 0.0), axis=1, keepdims=True)


def _sample_mixer_kernel(z_ref, cb_ref, s0_ref, cw_ref, lbl_ref, gn_ref, ab_ref, conv_ref, s_ref,
                         b_scr, *, layer_j, tb):
    u = z_ref[:, Z_C:Z_C + CONV_CH] * z_ref[:, Z_X:Z_X + CONV_CH]
    b0 = cb_ref[:, 0, :]
    b1 = cb_ref[:, 1, :]
    y = cw_ref[0:1, :] * b0 + cw_ref[1:2, :] * b1 + cw_ref[2:3, :] * u
    ab_ref[:, 0:CONV_CH] = (z_ref[:, Z_B:Z_B + CONV_CH] * y).astype(BF16)
    conv_ref[:, 0, :] = b1
    conv_ref[:, 1, :] = u

    lb_all = _hg_lower_bound(lbl_ref, layer_j)
    gn = gn_ref[...]
    eye = (lax.broadcasted_iota(jnp.int32, (HG_DK, HG_DK), 0)
           == lax.broadcasted_iota(jnp.int32, (HG_DK, HG_DK), 1))

    q_all, kk_all, logf_all = _hgrn_gates(z_ref[:, Z_Q:Z_Q + HG_WIDTH], z_ref[:, Z_F:Z_F + HG_WIDTH],
                                          lb_all)
    ef_all = jnp.exp(logf_all)
    v_all = z_ref[:, Z_I:Z_I + HG_WIDTH]
    for n in range(tb):
        for h in range(HG_HEADS):
            lo = h * HG_DK
            row = lambda a: a[n:n + 1, lo:lo + HG_DK]
            s_new = (_column(row(ef_all), eye) * s0_ref[n, h]
                     + _column(row(kk_all), eye) * row(v_all))
            s_ref[n, h] = s_new
            b_scr[n:n + 1, lo:lo + HG_DK] = jnp.sum(_column(row(q_all), eye) * s_new, axis=0,
                                                    keepdims=True)
    for h in range(HG_HEADS):
        lo = h * HG_DK
        b = _rmsnorm(b_scr[:, lo:lo + HG_DK], gn) * _silu(z_ref[:, Z_G + lo:Z_G + lo + HG_DK])
        ab_ref[:, CONV_CH + lo:CONV_CH + lo + HG_DK] = b.astype(BF16)


def _sample_mixer(z, conv_buf, s0, conv_w_j, lb_logits, gnorm_j, layer_j, tb):
    nb = z.shape[0]
    return pl.pallas_call(
        functools.partial(_sample_mixer_kernel, layer_j=layer_j, tb=tb),
        grid=(nb // tb,),
        in_specs=[
            pl.BlockSpec((tb, EVEN_IN), lambda i: (i, 0)),
            pl.BlockSpec((None, tb, CONV_W - 1, CONV_CH), lambda i: (layer_j, i, 0, 0)),
            pl.BlockSpec((None, tb, HG_HEADS, HG_DK, HG_DK), lambda i: (layer_j, i, 0, 0, 0)),
            pl.BlockSpec((CONV_W, CONV_CH), lambda i: (0, 0)),
            pl.BlockSpec(lb_logits.shape, lambda i: (0, 0)),
            pl.BlockSpec((1, HG_DK), lambda i: (0, 0)),
        ],
        out_specs=[
            pl.BlockSpec((tb, D_MODEL), lambda i: (i, 0)),
            pl.BlockSpec((tb, CONV_W - 1, CONV_CH), lambda i: (i, 0, 0)),
            pl.BlockSpec((tb, HG_HEADS, HG_DK, HG_DK), lambda i: (i, 0, 0, 0)),
        ],
        out_shape=[
            jax.ShapeDtypeStruct((nb, D_MODEL), BF16),
            jax.ShapeDtypeStruct((nb, CONV_W - 1, CONV_CH), F32),
            jax.ShapeDtypeStruct((nb, HG_HEADS, HG_DK, HG_DK), F32),
        ],
        scratch_shapes=[pltpu.VMEM((tb, HG_WIDTH), F32)],
        compiler_params=_cp(("parallel",)),
        name="sample_conv_hgrn",
    )(z, conv_buf, s0, conv_w_j, lb_logits, gnorm_j)


def _rope_table_kernel(cos_ref, sin_ref, *, tr, pos0, step):
    lane = lax.broadcasted_iota(jnp.int32, (tr, LANES), 1)
    rowi = lax.broadcasted_iota(jnp.int32, (tr, LANES), 0)
    half = ROPE_D // 2
    fi = (lane % half).astype(F32)
    inv = jnp.exp(fi * (-math.log(ROPE_THETA) / half))
    pos = (pos0 + step * (pl.program_id(0) * tr + rowi)).astype(F32)
    ang = pos * inv
    sign = jnp.where((lane % ROPE_D) < half, -1.0, 1.0)
    cos_ref[...] = jnp.cos(ang)
    sin_ref[...] = jnp.sin(ang) * sign


def _rope_table(rows, pos0, step, tr):
    return pl.pallas_call(
        functools.partial(_rope_table_kernel, tr=tr, pos0=pos0, step=step),
        grid=(rows // tr,),
        out_specs=[pl.BlockSpec((tr, LANES), lambda i: (i, 0))] * 2,
        out_shape=[jax.ShapeDtypeStruct((rows, LANES), F32)] * 2,
        compiler_params=_cp(("parallel",)),
        name="rope_table",
    )()


def _rope_pairs(g, cos, sin_signed):
    lane = lax.broadcasted_iota(jnp.int32, g.shape, 1)
    half = ROPE_D // 2
    n = g.shape[1]
    rot = jnp.where((lane % ROPE_D) < half, pltpu.roll(g, n - half, 1), pltpu.roll(g, half, 1))
    return g * cos + rot * sin_signed


def _dqkv_kernel(x_ref, g_ref, sh_ref, sc_ref, w_ref, qn_ref, kvn_ref, cos_ref, sin_ref,
                 cq_ref, ckv_ref, kr_ref, ckvb_ref, krb_ref):
    tm = x_ref.shape[0]
    sub = min(tm, 128)
    for r in range(0, tm, sub):
        rows = slice(r, r + sub)
        sc = sc_ref[...] if sc_ref.shape[0] == 1 else sc_ref[rows, :]
        sh = sh_ref[...] if sh_ref.shape[0] == 1 else sh_ref[rows, :]
        h = _rmsnorm(x_ref[rows, :], g_ref[...]) * (1.0 + sc) + sh
        d = _dot(h.astype(BF16), w_ref[...])
        cq_ref[rows, :] = _rmsnorm(d[:, :Q_LORA], qn_ref[...]).astype(BF16)
        ckv = _rmsnorm(d[:, Q_LORA:Q_LORA + KV_LORA], kvn_ref[...])
        ckv_ref[rows, :] = ckv
        ckvb_ref[rows, :] = ckv.astype(BF16)
        kr = _rope_pairs(d[:, Q_LORA + KV_LORA:], cos_ref[rows, :], sin_ref[rows, :])[:, :ROPE_D]
        kr_ref[rows, :] = kr
        krb_ref[rows, :] = kr.astype(BF16)


def _dqkv(x, g, mod, layer, w_pad, q_norm, kv_norm, cos_t, sin_t, tm, rope_blocks):
    m, d = x.shape
    n = w_pad.shape[1]
    rb = rope_blocks
    return pl.pallas_call(
        _dqkv_kernel,
        grid=(m // tm,),
        in_specs=[
            pl.BlockSpec((tm, d), lambda i: (i, 0)),
            pl.BlockSpec((1, d), lambda i: (0, 0)),
            mod.spec(layer, 0),
            mod.spec(layer, 1),
            pl.BlockSpec((d, n), lambda i: (0, 0)),
            pl.BlockSpec((1, Q_LORA), lambda i: (0, 0)),
            pl.BlockSpec((1, KV_LORA), lambda i: (0, 0)),
            pl.BlockSpec((cos_t.shape[0] // rb, LANES), lambda i: (i % rb, 0)),
            pl.BlockSpec((cos_t.shape[0] // rb, LANES), lambda i: (i % rb, 0)),
        ],
        out_specs=[
            pl.BlockSpec((tm, Q_LORA), lambda i: (i, 0)),
            pl.BlockSpec((tm, KV_LORA), lambda i: (i, 0)),
            pl.BlockSpec((tm, ROPE_D), lambda i: (i, 0)),
            pl.BlockSpec((tm, KV_LORA), lambda i: (i, 0)),
            pl.BlockSpec((tm, ROPE_D), lambda i: (i, 0)),
        ],
        out_shape=[
            jax.ShapeDtypeStruct((m, Q_LORA), BF16),
            jax.ShapeDtypeStruct((m, KV_LORA), F32),
            jax.ShapeDtypeStruct((m, ROPE_D), F32),
            jax.ShapeDtypeStruct((m, KV_LORA), BF16),
            jax.ShapeDtypeStruct((m, ROPE_D), BF16),
        ],
        compiler_params=_cp(("parallel",)),
        name="mla_down_proj",
    )(x, g, mod.arr, mod.arr, w_pad, q_norm, kv_norm, cos_t, sin_t)


def _q_kernel(cq_ref, wn_ref, wr_ref, wuk_ref, cos_ref, sin_ref, ql_ref, qr_ref):
    cq = cq_ref[...]
    qn = _dot(cq, wn_ref[...])
    qr = _dot(cq, wr_ref[...])
    cos = jnp.concatenate([cos_ref[...]] * (MLA_HEADS * ROPE_D // LANES), axis=1)
    sin = jnp.concatenate([sin_ref[...]] * (MLA_HEADS * ROPE_D // LANES), axis=1)
    qr = _rope_pairs(qr, cos, sin).astype(BF16)
    for h in range(MLA_HEADS):
        ql_ref[h] = _dot(qn[:, h * NOPE:(h + 1) * NOPE].astype(BF16), wuk_ref[h]).astype(BF16)
        qr_ref[h] = qr[:, h * ROPE_D:(h + 1) * ROPE_D]


def _q_proj(cq, w_nope, w_rope, w_ukt, cos_t, sin_t, nb, seq, tm, rope_blocks):
    nt = seq // tm
    rb = rope_blocks
    return pl.pallas_call(
        _q_kernel,
        grid=(nb, nt),
        in_specs=[
            pl.BlockSpec((tm, Q_LORA), lambda n, t: (n * nt + t, 0)),
            pl.BlockSpec(w_nope.shape, lambda n, t: (0, 0)),
            pl.BlockSpec(w_rope.shape, lambda n, t: (0, 0)),
            pl.BlockSpec(w_ukt.shape, lambda n, t: (0, 0, 0)),
            pl.BlockSpec((cos_t.shape[0] // rb, LANES), lambda n, t: (t % rb, 0)),
            pl.BlockSpec((cos_t.shape[0] // rb, LANES), lambda n, t: (t % rb, 0)),
        ],
        out_specs=[
            pl.BlockSpec((None, MLA_HEADS, tm, KV_LORA), lambda n, t: (n, 0, t, 0)),
            pl.BlockSpec((None, MLA_HEADS, tm, ROPE_D), lambda n, t: (n, 0, t, 0)),
        ],
        out_shape=[
            jax.ShapeDtypeStruct((nb, MLA_HEADS, seq, KV_LORA), BF16),
            jax.ShapeDtypeStruct((nb, MLA_HEADS, seq, ROPE_D), BF16),
        ],
        compiler_params=_cp(("parallel", "parallel")),
        name="mla_q_proj",
    )(cq, w_nope, w_rope, w_ukt, cos_t, sin_t)


def _lane_tile(x, width):
    return x if width == LANES else jnp.concatenate([x] * (width // LANES), axis=1)


def _prompt_attn_kernel(ql_ref, qr_ref, k_ref, kr_ref, o_ref, m_scr, l_scr, acc_scr, *, tq):
    i = pl.program_id(1)
    m_scr[...] = jnp.full_like(m_scr, NEG)
    l_scr[...] = jnp.zeros_like(l_scr)
    acc_scr[...] = jnp.zeros_like(acc_scr)

    def block(j, masked):
        r0 = pl.multiple_of(j * tq, tq)
        k = k_ref[pl.ds(r0, tq), :]
        kr = kr_ref[pl.ds(r0, tq), :]
        if masked:
            causal = (lax.broadcasted_iota(jnp.int32, (tq, tq), 1)
                      <= lax.broadcasted_iota(jnp.int32, (tq, tq), 0))

        def head(h, carry):
            s = (_dot_nt(ql_ref[h], k) + _dot_nt(qr_ref[h], kr)) * (MLA_SCALE * LOG2E)
            if masked:
                s = jnp.where(causal, s, NEG)
            m_prev = m_scr[h]
            m_new = jnp.maximum(m_prev, jnp.max(s, axis=-1, keepdims=True))
            alpha = jnp.exp2(m_prev - m_new)
            p = jnp.exp2(s - _lane_tile(m_new, tq))
            l_scr[h] = alpha * l_scr[h] + jnp.sum(p, axis=-1, keepdims=True)
            acc_scr[h] = _lane_tile(alpha, KV_LORA) * acc_scr[h] + _dot(p.astype(BF16), k)
            m_scr[h] = m_new
            return carry

        lax.fori_loop(0, MLA_HEADS, head, 0, unroll=True)

    def body(j, carry):
        block(j, False)
        return carry

    lax.fori_loop(0, i, body, 0)
    block(i, True)
    for h in range(MLA_HEADS):
        ctx = acc_scr[h] / _lane_tile(l_scr[h], KV_LORA)
        o_ref[:, h * KV_LORA:(h + 1) * KV_LORA] = ctx.astype(BF16)


def _prompt_attn(q_lat, q_rope, ckv_b, kr_b, nb, seq, tq):
    nq = seq // tq
    return pl.pallas_call(
        functools.partial(_prompt_attn_kernel, tq=tq),
        grid=(nb, nq),
        in_specs=[
            pl.BlockSpec((None, MLA_HEADS, tq, KV_LORA), lambda n, i: (n, 0, i, 0)),
            pl.BlockSpec((None, MLA_HEADS, tq, ROPE_D), lambda n, i: (n, 0, i, 0)),
            pl.BlockSpec((seq, KV_LORA), lambda n, i: (n, 0)),
            pl.BlockSpec((seq, ROPE_D), lambda n, i: (n, 0)),
        ],
        out_specs=pl.BlockSpec((tq, MLA_HEADS * KV_LORA), lambda n, i: (n * nq + i, 0)),
        out_shape=jax.ShapeDtypeStruct((nb * seq, MLA_HEADS * KV_LORA), BF16),
        scratch_shapes=[pltpu.VMEM((MLA_HEADS, tq, LANES), F32),
                        pltpu.VMEM((MLA_HEADS, tq, LANES), F32),
                        pltpu.VMEM((MLA_HEADS, tq, KV_LORA), F32)],
        compiler_params=_cp(("parallel", "arbitrary")),
        name="mla_prompt_attn",
    )(q_lat, q_rope, ckv_b, kr_b)


def _sample_attn_kernel(pt_ref, ql_ref, qr_ref, cn_ref, krn_ref, ckv_hbm, krt_hbm, o_ref,
                        kbuf, rbuf, kb16, kt16, sems, *, layer_j, n_pages):
    b = pl.program_id(0)
    nb = pl.num_programs(0)
    slot = b % 2

    def page_copies(seq, slot_, p):
        page = pt_ref[seq, p]
        return (
            pltpu.make_async_copy(ckv_hbm.at[layer_j, page],
                                  kbuf.at[slot_, p * PAGE_SIZE:(p + 1) * PAGE_SIZE, :],
                                  sems.at[0, slot_]),
            pltpu.make_async_copy(krt_hbm.at[layer_j, page],
                                  rbuf.at[slot_, :, p * PAGE_SIZE:(p + 1) * PAGE_SIZE],
                                  sems.at[1, slot_]),
        )

    def start_all(seq, slot_):
        for p in range(n_pages):
            for cp in page_copies(seq, slot_, p):
                cp.start()

    @pl.when(b == 0)
    def _():
        start_all(0, 0)

    @pl.when(b + 1 < nb)
    def _():
        start_all(b + 1, 1 - slot)

    for p in range(n_pages):
        for cp in page_copies(b, slot, p):
            cp.wait()

    ql = ql_ref[...]
    qr = qr_ref[...]
    past = n_pages * PAGE_SIZE
    tc = min(past, 1024)
    for c in range(past // tc):
        kb = kbuf[slot, c * tc:(c + 1) * tc, :].astype(BF16)
        kb16[c * tc:(c + 1) * tc, :] = kb
        kt16[:, c * tc:(c + 1) * tc] = kb.T
    s = (_dot(ql, kt16[...]) + _dot(qr, rbuf[slot].astype(BF16))) * MLA_SCALE
    cn = cn_ref[...]
    s_new = (jnp.sum(ql.astype(F32) * cn, axis=-1, keepdims=True)
             + jnp.sum(qr.astype(F32) * krn_ref[...], axis=-1, keepdims=True)) * MLA_SCALE
    m = jnp.maximum(jnp.max(s, axis=-1, keepdims=True), s_new)
    p = jnp.exp(s - m)
    p_new = jnp.exp(s_new - m)
    l = jnp.sum(p, axis=-1, keepdims=True) + p_new
    acc = _dot(p.astype(BF16), kb16[...]) + p_new * cn
    o_ref[...] = (acc / l).astype(BF16)


def _sample_attn(page_table, q_lat, q_rope, ckv_new, kr_new, cache_ckv, cache_krope_t, layer_j):
    nb, n_pages = page_table.shape
    past = n_pages * PAGE_SIZE
    grid_spec = pltpu.PrefetchScalarGridSpec(
        num_scalar_prefetch=1,
        grid=(nb,),
        in_specs=[
            pl.BlockSpec((None, MLA_HEADS, KV_LORA), lambda b, pt: (b, 0, 0)),
            pl.BlockSpec((None, MLA_HEADS, ROPE_D), lambda b, pt: (b, 0, 0)),
            pl.BlockSpec((None, 1, KV_LORA), lambda b, pt: (b, 0, 0)),
            pl.BlockSpec((None, 1, ROPE_D), lambda b, pt: (b, 0, 0)),
            pl.BlockSpec(memory_space=pl.ANY),
            pl.BlockSpec(memory_space=pl.ANY),
        ],
        out_specs=pl.BlockSpec((None, MLA_HEADS, KV_LORA), lambda b, pt: (b, 0, 0)),
        scratch_shapes=[
            pltpu.VMEM((2, past, KV_LORA), F32),
            pltpu.VMEM((2, ROPE_D, past), F32),
            pltpu.VMEM((past, KV_LORA), BF16),
            pltpu.VMEM((KV_LORA, past), BF16),
            pltpu.SemaphoreType.DMA((2, 2)),
        ],
    )
    return pl.pallas_call(
        functools.partial(_sample_attn_kernel, layer_j=layer_j, n_pages=n_pages),
        grid_spec=grid_spec,
        out_shape=jax.ShapeDtypeStruct((nb, MLA_HEADS, KV_LORA), BF16),
        compiler_params=_cp(("arbitrary",)),
        name="mla_sample_attn",
    )(page_table, q_lat, q_rope, ckv_new.reshape(nb, 1, KV_LORA), kr_new.reshape(nb, 1, ROPE_D),
      cache_ckv, cache_krope_t)


def _tile(m, pref):
    t = min(m, pref)
    assert m % t == 0, (m, t)
    return t


def kernel(x_prompt, x_sample, c_prompt, c_sample, cache_conv, state_hgrn, cache_ckv, cache_krope,
           page_table, w_ada, b_ada, norm_mix, norm_ffn, w_in_even, conv_w, hg_lb_logits, hg_gnorm,
           w_out_even, w_dqkv, q_norm, w_uq, kv_norm, w_uk, w_uv, w_o, w_gu, w_down, norm_final):
    bp, tp, d = x_prompt.shape
    bs, ts, _ = x_sample.shape
    assert ts == 1 and d == D_MODEL
    depth = w_ada.shape[0]
    n_pages = page_table.shape[1]
    past_len = n_pages * PAGE_SIZE
    mp, ms = bp * tp, bs

    tm_p = _tile(tp, 512)
    tm_s = ms
    tt = _tile(tp, 256)
    tq = _tile(tp, 512)
    tb = _tile(bs, 16)
    ns = 1
    krope_t = jnp.swapaxes(cache_krope, 2, 3)
    fc = 256

    mods = _ada_all(jnp.concatenate([c_prompt, c_sample], axis=0), w_ada, b_ada)
    mod_p = _Mod(mods[:, :bp].reshape(depth, bp, 1, N_MOD * d), tp // tm_p)
    mod_s = _Mod(mods[:, bp:].reshape(depth, 1, bs, N_MOD * d), 1)

    cos_p, sin_p = _rope_table(tp, 0, 1, _tile(tp, 512))
    cos_s, sin_s = _rope_table(8, past_len, 0, 8)
    cos_s = jnp.broadcast_to(cos_s[:1], (ms, LANES))
    sin_s = jnp.broadcast_to(sin_s[:1], (ms, LANES))

    w_in_b = w_in_even.astype(BF16)
    w_out_b = w_out_even.astype(BF16)
    w_gu_b = w_gu.astype(BF16)
    w_down_b = w_down.astype(BF16)
    w_o_b = w_o.astype(BF16)
    w_uvh_b = jnp.transpose(w_uv, (0, 2, 1, 3)).astype(BF16)

    xp = x_prompt.reshape(mp, d)
    xs = x_sample.reshape(ms, d)
    outs ={k: [] for k in ("conv_p", "hg_p", "ckv_p", "kr_p", "conv_s", "hg_s", "ckv_s", "kr_s")}

    for l in range(depth):
        j = l // 2
        g_mix = norm_mix[l].reshape(1, d)
        g_ffn = norm_ffn[l].reshape(1, d)
        if l % 2 == 0:
            gn = hg_gnorm[j].reshape(1, HG_DK)
            zp = _mod_matmul(xp, g_mix, mod_p, l, w_in_b, j, tm_p)
            zs = _mod_matmul(xs, g_mix, mod_s, l, w_in_b, j, tm_s)
            abp, cvp, hgp = _prompt_mixer(zp.reshape(bp, tp, EVEN_IN), conv_w[j], hg_lb_logits, gn,
                                          j, bp, tp, tt, ns)
            abp = abp.reshape(mp, d)
            abs_, cvs, hgs = _sample_mixer(zs, cache_conv, state_hgrn, conv_w[j], hg_lb_logits,
                                           gn, j, tb)
            mix_p, mix_s = (abp, w_out_b), (abs_, w_out_b)
            mix_w_specs = [_resident(w_out_b, j)]
            outs["conv_p"].append(cvp)
            outs["hg_p"].append(hgp)
            outs["conv_s"].append(cvs)
            outs["hg_s"].append(hgs)
        else:
            w_d = jnp.pad(w_dqkv[j], ((0, 0), (0, ROPE_D))).astype(BF16)
            wq = w_uq[j].reshape(Q_LORA, MLA_HEADS, NOPE + ROPE_D)
            w_nope = wq[:, :, :NOPE].reshape(Q_LORA, MLA_HEADS * NOPE).astype(BF16)
            w_rope = wq[:, :, NOPE:].reshape(Q_LORA, MLA_HEADS * ROPE_D).astype(BF16)
            w_ukt = jnp.transpose(w_uk[j], (1, 2, 0)).astype(BF16)
            qn = q_norm[j].reshape(1, Q_LORA)
            kvn = kv_norm[j].reshape(1, KV_LORA)

            cqp, ckvp, krp, ckvpb, krpb = _dqkv(xp, g_mix, mod_p, l, w_d, qn, kvn, cos_p, sin_p,
                                                tm_p, tp // tm_p)
            cqs, ckvs, krs, _, _ = _dqkv(xs, g_mix, mod_s, l, w_d, qn, kvn, cos_s, sin_s, tm_s, 1)
            qlp, qrp = _q_proj(cqp, w_nope, w_rope, w_ukt, cos_p, sin_p, bp, tp, tm_p, tp // tm_p)
            qls, qrs = _q_proj(cqs, w_nope, w_rope, w_ukt, cos_s, sin_s, 1, ms, tm_s, 1)
            ctxp = _prompt_attn(qlp, qrp, ckvpb, krpb, bp, tp, tq)
            ctxs = _sample_attn(page_table,
                                jnp.transpose(qls[0], (1, 0, 2)), jnp.transpose(qrs[0], (1, 0, 2)),
                                ckvs, krs, cache_ckv, krope_t, j)
            mix_p = (ctxp, w_uvh_b, w_o_b)
            mix_s = (ctxs.reshape(ms, MLA_HEADS * KV_LORA), w_uvh_b, w_o_b)
            mix_w_specs = [_resident(w_uvh_b, j), _resident(w_o_b, j)]
            outs["ckv_p"].append(ckvp.reshape(bp, tp, KV_LORA))
            outs["kr_p"].append(krp.reshape(bp, tp, ROPE_D))
            outs["ckv_s"].append(ckvs.reshape(bs, ts, KV_LORA))
            outs["kr_s"].append(krs.reshape(bs, ts, ROPE_D))

        last = l == depth - 1
        gfin = norm_final.reshape(1, d)
        def row_spec(a, tm):
            return pl.BlockSpec((tm, a.shape[1]), lambda i: (i, 0))

        xp = _ffn(mix_p, [row_spec(mix_p[0], tm_p)] + mix_w_specs, xp, g_ffn, mod_p, l,
                  w_gu_b, w_down_b, gfin, last, tm_p, fc)
        xs = _ffn(mix_s, [row_spec(mix_s[0], tm_s)] + mix_w_specs, xs, g_ffn, mod_s, l,
                  w_gu_b, w_down_b, gfin, last, tm_s, fc)

    return (xp.reshape(bp, tp, d), xs.reshape(bs, ts, d),
            jnp.stack(outs["conv_p"]), jnp.stack(outs["hg_p"]),
            jnp.stack(outs["ckv_p"]), jnp.stack(outs["kr_p"]),
            jnp.stack(outs["conv_s"]), jnp.stack(outs["hg_s"]),
            jnp.stack(outs["ckv_s"]), jnp.stack(outs["kr_s"]))
```

```python
import functools
import math

import jax
import jax.numpy as jnp
from jax import lax
from jax.experimental import pallas as pl
from jax.experimental.pallas import tpu as pltpu

F32 = jnp.float32
BF16 = jnp.bfloat16

D_MODEL = 1024
N_MOD = 6
EPS = 1e-6
NEG = -1e30
PAGE_SIZE = 128
CONV_CH = D_MODEL // 2
CONV_W = 3
HG_WIDTH = D_MODEL // 2
HG_DK = 128
HG_HEADS = HG_WIDTH // HG_DK
EVEN_IN = 3 * CONV_CH + 4 * HG_WIDTH
MLA_HEADS = 8
NOPE = 128
ROPE_D = 64
V_D = 128
Q_LORA = 384
KV_LORA = 256
ROPE_THETA = 10000.0
MLA_SCALE = (NOPE + ROPE_D) ** -0.5
HEAD_QK = 256
LANES = 128
SUBLANES = 8
LOG2E = 1.0 / math.log(2.0)

Z_B, Z_C, Z_X = 0, CONV_CH, 2 * CONV_CH
Z_Q = 3 * CONV_CH
Z_F = Z_Q + HG_WIDTH
Z_I = Z_F + HG_WIDTH
Z_G = Z_I + HG_WIDTH

HG_SUB = 16
HG_UNROLL = 2
VMEM_LIMIT = 56 * 1024 * 1024


def _cp(sem, vmem=VMEM_LIMIT):
    return pltpu.CompilerParams(dimension_semantics=sem, vmem_limit_bytes=vmem)


def _silu(x):
    return x * jax.nn.sigmoid(x)


def _rmsnorm(x, g):
    return x * lax.rsqrt(jnp.mean(x * x, axis=-1, keepdims=True) + EPS) * g


def _dot(a, b):
    return jnp.dot(a, b, preferred_element_type=F32)


def _dot_nt(a, b):
    return lax.dot_general(a, b, (((1,), (1,)), ((), ())), preferred_element_type=F32)


def _dot_tn(a, b):
    return lax.dot_general(a, b, (((0,), (0,)), ((), ())), preferred_element_type=F32)


def _ada_kernel(c_ref, w_ref, b_ref, o_ref):
    a = _silu(c_ref[...]).astype(BF16)
    o_ref[...] = _dot(a, w_ref[...].astype(BF16)) + b_ref[...]


def _ada_all(c_all, w_ada, b_ada, tn=1536):
    depth, d, n6 = w_ada.shape
    rows = c_all.shape[0]
    return pl.pallas_call(
        _ada_kernel,
        grid=(depth, n6 // tn),
        in_specs=[
            pl.BlockSpec((rows, d), lambda l, j: (0, 0)),
            pl.BlockSpec((None, d, tn), lambda l, j: (l, 0, j)),
            pl.BlockSpec((None, 1, tn), lambda l, j: (l, 0, j)),
        ],
        out_specs=pl.BlockSpec((None, rows, tn), lambda l, j: (l, 0, j)),
        out_shape=jax.ShapeDtypeStruct((depth, rows, n6), F32),
        compiler_params=_cp(("parallel", "parallel")),
        name="adaln_mod",
    )(c_all, w_ada, b_ada.reshape(depth, 1, n6))


class _Mod:
    def __init__(self, arr, tps):
        self.arr = arr
        self.tps = tps
        self.r = arr.shape[2]

    def spec(self, layer, col, width=D_MODEL, ncol=None):
        tps = self.tps
        per = D_MODEL // width
        if ncol is None:
            return pl.BlockSpec((None, None, self.r, width),
                                lambda i, *_: (layer, i // tps, 0, col * per))
        return pl.BlockSpec((None, None, self.r, width),
                            lambda i, j, *_: (layer, i // tps, 0, col * per + j))


def _resident(stacked, layer):
    shape = stacked.shape[1:]
    return pl.BlockSpec((None,) + shape, lambda *_: (layer,) + (0,) * len(shape),
                        pipeline_mode=pl.Buffered(1))


def _modmm_kernel(x_ref, g_ref, sh_ref, sc_ref, w_ref, o_ref):
    h = _rmsnorm(x_ref[...], g_ref[...]) * (1.0 + sc_ref[...]) + sh_ref[...]
    o_ref[...] = _dot(h.astype(BF16), w_ref[...])


def _mod_matmul(x, g, mod, layer, w_all, j, tm):
    m, d = x.shape
    n = w_all.shape[2]
    return pl.pallas_call(
        _modmm_kernel,
        grid=(m // tm,),
        in_specs=[
            pl.BlockSpec((tm, d), lambda i: (i, 0)),
            pl.BlockSpec((1, d), lambda i: (0, 0)),
            mod.spec(layer, 0),
            mod.spec(layer, 1),
            _resident(w_all, j),
        ],
        out_specs=pl.BlockSpec((tm, n), lambda i: (i, 0)),
        out_shape=jax.ShapeDtypeStruct((m, n), F32),
        compiler_params=_cp(("parallel",)),
        name="mod_matmul",
    )(x, g, mod.arr, mod.arr, w_all)


def _ffn_kernel(*refs, final_norm, dff, fc, n_mix):
    mix = refs[:n_mix]
    (x_ref, gmix_ref, g_ref, sh_ref, sc_ref, gate_ref, wgu_ref, wd_ref, gf_ref, o_ref,
     a_scr) = refs[n_mix:]
    if n_mix == 2:
        a_ref, wout_ref = mix
        y_mix = _dot(a_ref[...], wout_ref[...])
    else:
        ctx_ref, wuv_ref, wo_ref = mix
        parts = [_dot(ctx_ref[:, h * KV_LORA:(h + 1) * KV_LORA], wuv_ref[h]).astype(BF16)
                 for h in range(MLA_HEADS)]
        y_mix = _dot(jnp.concatenate(parts, axis=1), wo_ref[...])
    x = x_ref[...] + gmix_ref[...] * y_mix
    h = (_rmsnorm(x, g_ref[...]) * (1.0 + sc_ref[...]) + sh_ref[...]).astype(BF16)
    for c in range(dff // fc):
        gg = _dot(h, wgu_ref[:, c * fc:(c + 1) * fc])
        uu = _dot(h, wgu_ref[:, dff + c * fc:dff + (c + 1) * fc])
        a_scr[:, c * fc:(c + 1) * fc] = (_silu(gg) * uu).astype(BF16)
    y = x + gate_ref[...] * _dot(a_scr[...], wd_ref[...])
    if final_norm:
        y = _rmsnorm(y, gf_ref[...])
    o_ref[...] = y


def _ffn(mix, mix_specs, x, g, mod, layer, w_gu, w_down, g_final, final_norm, tm, fc):
    m, d = x.shape
    dff = w_down.shape[1]
    return pl.pallas_call(
        functools.partial(_ffn_kernel, final_norm=final_norm, dff=dff, fc=fc, n_mix=len(mix)),
        grid=(m // tm,),
        in_specs=list(mix_specs) + [
            pl.BlockSpec((tm, d), lambda i: (i, 0)),
            mod.spec(layer, 2),
            pl.BlockSpec((1, d), lambda i: (0, 0)),
            mod.spec(layer, 3),
            mod.spec(layer, 4),
            mod.spec(layer, 5),
            _resident(w_gu, layer),
            _resident(w_down, layer),
            pl.BlockSpec((1, d), lambda i: (0, 0)),
        ],
        out_specs=pl.BlockSpec((tm, d), lambda i: (i, 0)),
        out_shape=jax.ShapeDtypeStruct((m, d), F32),
        scratch_shapes=[pltpu.VMEM((tm, dff), BF16)],
        compiler_params=_cp(("parallel",)),
        name="mix_out_ffn",
    )(*mix, x, mod.arr, g, mod.arr, mod.arr, mod.arr, w_gu, w_down, g_final)


def _hg_lower_bound(lbl_ref, j):
    logits = lbl_ref[...]
    e = jnp.exp(logits - jnp.max(logits, axis=0, keepdims=True))
    den = jnp.sum(e, axis=0, keepdims=True)
    lb = jnp.zeros_like(den)
    for i in range(j):
        lb = lb + e[i:i + 1, :] / den
    return lb


def _cumsum_rows(x):
    rows = x.shape[0]
    idx = lax.broadcasted_iota(jnp.int32, x.shape, 0)
    d = 1
    while d < rows:
        x = x + jnp.where(idx >= d, pltpu.roll(x, d, 0), 0.0)
        d *= 2
    return x


def _hgrn_gates(qp, fp, lb):
    logf = jnp.log(lb + (1.0 - lb) * jax.nn.sigmoid(fp))
    kk = (1.0 - lb) * jax.nn.sigmoid(-fp)
    q = _silu(qp) * (HG_DK ** -0.5)
    return q, kk, logf


def _prompt_mixer_kernel(z_ref, cw_ref, lbl_ref, gn_ref, ab_ref, conv_ref, s_ref, ubuf, st_scr,
                         *, layer_j, tt, ns):
    t = pl.program_id(1)
    nt = pl.num_programs(1)

    @pl.when(t == 0)
    def _():
        ubuf[:, 0:8, :] = jnp.zeros((ns, 8, CONV_CH), F32)
        st_scr[...] = jnp.zeros_like(st_scr)

    for n in range(ns):
        u = z_ref[n, :, Z_C:Z_C + CONV_CH] * z_ref[n, :, Z_X:Z_X + CONV_CH]
        ubuf[n, 8:8 + tt, :] = u
        y = (cw_ref[0:1, :] * ubuf[n, 6:6 + tt, :] + cw_ref[1:2, :] * ubuf[n, 7:7 + tt, :]
             + cw_ref[2:3, :] * ubuf[n, 8:8 + tt, :])
        ab_ref[n, :, 0:CONV_CH] = (z_ref[n, :, Z_B:Z_B + CONV_CH] * y).astype(BF16)
        last2 = ubuf[n, tt + 6:tt + 8, :]
        ubuf[n, 6:8, :] = last2
        conv_ref[n] = last2

    lb_all = _hg_lower_bound(lbl_ref, layer_j)
    gn = gn_ref[...]
    ell = HG_SUB
    row8 = lax.broadcasted_iota(jnp.int32, (SUBLANES, HG_DK), 0)

    def chunk(c, carry):
        r0 = pl.multiple_of(c * ell, ell)
        for n in range(ns):
            for h in range(HG_HEADS):
                lo = h * HG_DK
                lb = lb_all[:, lo:lo + HG_DK]
                qp = z_ref[n, pl.ds(r0, ell), Z_Q + lo:Z_Q + lo + HG_DK]
                fp = z_ref[n, pl.ds(r0, ell), Z_F + lo:Z_F + lo + HG_DK]
                v = z_ref[n, pl.ds(r0, ell), Z_I + lo:Z_I + lo + HG_DK]
                gp = z_ref[n, pl.ds(r0, ell), Z_G + lo:Z_G + lo + HG_DK]
                q, kk, logf = _hgrn_gates(qp, fp, lb)
                gc = _cumsum_rows(logf)
                gl = gc[ell - 1:ell, :]
                st = st_scr[n, h]
                o = _dot_nt((q * jnp.exp(gc)).astype(BF16), st.astype(BF16))
                o_grp = [o[r:r + SUBLANES, :] for r in range(0, ell, SUBLANES)]
                gc2 = gc * LOG2E
                for s in range(ell):
                    for gi, r in enumerate(range(0, ell, SUBLANES)):
                        if r + SUBLANES <= s:
                            continue
                        d = gc2[r:r + SUBLANES, :] - gc2[s:s + 1, :]
                        if r <= s:
                            d = jnp.where(row8 >= s - r, d, NEG)
                        w = q[r:r + SUBLANES, :] * kk[s:s + 1, :] * jnp.exp2(d)
                        o_grp[gi] = (o_grp[gi]
                                     + jnp.sum(w, axis=-1, keepdims=True) * v[s:s + 1, :])
                o = jnp.concatenate(o_grp, axis=0)
                kd = kk * jnp.exp(gl - gc)
                st_scr[n, h] = st * jnp.exp(gl) + _dot_tn(v.astype(BF16), kd.astype(BF16))
                b = _rmsnorm(o, gn) * _silu(gp)
                ab_ref[n, pl.ds(r0, ell), CONV_CH + lo:CONV_CH + lo + HG_DK] = b.astype(BF16)
        return carry

    lax.fori_loop(0, tt // ell, chunk, 0, unroll=HG_UNROLL)

    @pl.when(t == nt - 1)
    def _():
        for n in range(ns):
            for h in range(HG_HEADS):
                s_ref[n, h] = st_scr[n, h].T


def _prompt_mixer(z, conv_w_j, lb_logits, gnorm_j, layer_j, nb, seq, tt, ns):
    nt = seq // tt
    return pl.pallas_call(
        functools.partial(_prompt_mixer_kernel, layer_j=layer_j, tt=tt, ns=ns),
        grid=(nb // ns, nt),
        in_specs=[
            pl.BlockSpec((ns, tt, EVEN_IN), lambda g, t: (g, t, 0)),
            pl.BlockSpec((CONV_W, CONV_CH), lambda g, t: (0, 0)),
            pl.BlockSpec(lb_logits.shape, lambda g, t: (0, 0)),
            pl.BlockSpec((1, HG_DK), lambda g, t: (0, 0)),
        ],
        out_specs=[
            pl.BlockSpec((ns, tt, D_MODEL), lambda g, t: (g, t, 0)),
            pl.BlockSpec((ns, CONV_W - 1, CONV_CH), lambda g, t: (g, 0, 0)),
            pl.BlockSpec((ns, HG_HEADS, HG_DK, HG_DK), lambda g, t: (g, 0, 0, 0)),
        ],
        out_shape=[
            jax.ShapeDtypeStruct((nb, seq, D_MODEL), BF16),
            jax.ShapeDtypeStruct((nb, CONV_W - 1, CONV_CH), F32),
            jax.ShapeDtypeStruct((nb, HG_HEADS, HG_DK, HG_DK), F32),
        ],
        scratch_shapes=[pltpu.VMEM((ns, tt + 8, CONV_CH), F32),
                        pltpu.VMEM((ns, HG_HEADS, HG_DK, HG_DK), F32)],
        compiler_params=_cp(("parallel", "arbitrary")),
        name="prompt_conv_hgrn",
    )(z, conv_w_j, lb_logits, gnorm_j)


def _column(row_vec, eye):
    return jnp.sum(jnp.where(eye, row_vec, 0.0), axis=1, keepdims=True)


def _sample_mixer_kernel(z_ref, cb_ref, s0_ref, cw_ref, lbl_ref, gn_ref, ab_ref, conv_ref, s_ref,
                         b_scr, *, layer_j, tb):
    u = z_ref[:, Z_C:Z_C + CONV_CH] * z_ref[:, Z_X:Z_X + CONV_CH]
    b0 = cb_ref[:, 0, :]
    b1 = cb_ref[:, 1, :]
    y = cw_ref[0:1, :] * b0 + cw_ref[1:2, :] * b1 + cw_ref[2:3, :] * u
    ab_ref[:, 0:CONV_CH] = (z_ref[:, Z_B:Z_B + CONV_CH] * y).astype(BF16)
    conv_ref[:, 0, :] = b1
    conv_ref[:, 1, :] = u

    lb_all = _hg_lower_bound(lbl_ref, layer_j)
    gn = gn_ref[...]
    eye = (lax.broadcasted_iota(jnp.int32, (HG_DK, HG_DK), 0)
           == lax.broadcasted_iota(jnp.int32, (HG_DK, HG_DK), 1))

    q_all, kk_all, logf_all = _hgrn_gates(z_ref[:, Z_Q:Z_Q + HG_WIDTH], z_ref[:, Z_F:Z_F + HG_WIDTH],
                                          lb_all)
    ef_all = jnp.exp(logf_all)
    v_all = z_ref[:, Z_I:Z_I + HG_WIDTH]
    for n in range(tb):
        for h in range(HG_HEADS):
            lo = h * HG_DK
            row = lambda a: a[n:n + 1, lo:lo + HG_DK]
            s_new = (_column(row(ef_all), eye) * s0_ref[n, h]
                     + _column(row(kk_all), eye) * row(v_all))
            s_ref[n, h] = s_new
            b_scr[n:n + 1, lo:lo + HG_DK] = jnp.sum(_column(row(q_all), eye) * s_new, axis=0,
                                                    keepdims=True)
    for h in range(HG_HEADS):
        lo = h * HG_DK
        b = _rmsnorm(b_scr[:, lo:lo + HG_DK], gn) * _silu(z_ref[:, Z_G + lo:Z_G + lo + HG_DK])
        ab_ref[:, CONV_CH + lo:CONV_CH + lo + HG_DK] = b.astype(BF16)


def _sample_mixer(z, conv_buf, s0, conv_w_j, lb_logits, gnorm_j, layer_j, tb):
    nb = z.shape[0]
    return pl.pallas_call(
        functools.partial(_sample_mixer_kernel, layer_j=layer_j, tb=tb),
        grid=(nb // tb,),
        in_specs=[
            pl.BlockSpec((tb, EVEN_IN), lambda i: (i, 0)),
            pl.BlockSpec((None, tb, CONV_W - 1, CONV_CH), lambda i: (layer_j, i, 0, 0)),
            pl.BlockSpec((None, tb, HG_HEADS, HG_DK, HG_DK), lambda i: (layer_j, i, 0, 0, 0)),
            pl.BlockSpec((CONV_W, CONV_CH), lambda i: (0, 0)),
            pl.BlockSpec(lb_logits.shape, lambda i: (0, 0)),
            pl.BlockSpec((1, HG_DK), lambda i: (0, 0)),
        ],
        out_specs=[
            pl.BlockSpec((tb, D_MODEL), lambda i: (i, 0)),
            pl.BlockSpec((tb, CONV_W - 1, CONV_CH), lambda i: (i, 0, 0)),
            pl.BlockSpec((tb, HG_HEADS, HG_DK, HG_DK), lambda i: (i, 0, 0, 0)),
        ],
        out_shape=[
            jax.ShapeDtypeStruct((nb, D_MODEL), BF16),
            jax.ShapeDtypeStruct((nb, CONV_W - 1, CONV_CH), F32),
            jax.ShapeDtypeStruct((nb, HG_HEADS, HG_DK, HG_DK), F32),
        ],
        scratch_shapes=[pltpu.VMEM((tb, HG_WIDTH), F32)],
        compiler_params=_cp(("parallel",)),
        name="sample_conv_hgrn",
    )(z, conv_buf, s0, conv_w_j, lb_logits, gnorm_j)


def _rope_table_kernel(cos_ref, sin_ref, *, tr, pos0, step):
    lane = lax.broadcasted_iota(jnp.int32, (tr, LANES), 1)
    rowi = lax.broadcasted_iota(jnp.int32, (tr, LANES), 0)
    half = ROPE_D // 2
    fi = (lane % half).astype(F32)
    inv = jnp.exp(fi * (-math.log(ROPE_THETA) / half))
    pos = (pos0 + step * (pl.program_id(0) * tr + rowi)).astype(F32)
    ang = pos * inv
    sign = jnp.where((lane % ROPE_D) < half, -1.0, 1.0)
    cos_ref[...] = jnp.cos(ang)
    sin_ref[...] = jnp.sin(ang) * sign


def _rope_table(rows, pos0, step, tr):
    return pl.pallas_call(
        functools.partial(_rope_table_kernel, tr=tr, pos0=pos0, step=step),
        grid=(rows // tr,),
        out_specs=[pl.BlockSpec((tr, LANES), lambda i: (i, 0))] * 2,
        out_shape=[jax.ShapeDtypeStruct((rows, LANES), F32)] * 2,
        compiler_params=_cp(("parallel",)),
        name="rope_table",
    )()


def _rope_pairs(g, cos, sin_signed):
    lane = lax.broadcasted_iota(jnp.int32, g.shape, 1)
    half = ROPE_D // 2
    n = g.shape[1]
    rot = jnp.where((lane % ROPE_D) < half, pltpu.roll(g, n - half, 1), pltpu.roll(g, half, 1))
    return g * cos + rot * sin_signed


def _dqkv_kernel(x_ref, g_ref, sh_ref, sc_ref, w_ref, qn_ref, kvn_ref, cos_ref, sin_ref, *rest,
                 per_head_kv):
    if per_head_kv:
        wuk_ref, wuv_ref, cq_ref, ckv_ref, kr_ref, kc_ref, v_ref = rest
    else:
        cq_ref, ckv_ref, kr_ref = rest
    tm = x_ref.shape[0]
    sub = min(tm, 128)
    for r in range(0, tm, sub):
        rows = slice(r, r + sub)
        sc = sc_ref[...] if sc_ref.shape[0] == 1 else sc_ref[rows, :]
        sh = sh_ref[...] if sh_ref.shape[0] == 1 else sh_ref[rows, :]
        h = _rmsnorm(x_ref[rows, :], g_ref[...]) * (1.0 + sc) + sh
        d = _dot(h.astype(BF16), w_ref[...])
        cq_ref[rows, :] = _rmsnorm(d[:, :Q_LORA], qn_ref[...]).astype(BF16)
        ckv = _rmsnorm(d[:, Q_LORA:Q_LORA + KV_LORA], kvn_ref[...])
        ckv_ref[rows, :] = ckv
        kr = _rope_pairs(d[:, Q_LORA + KV_LORA:], cos_ref[rows, :], sin_ref[rows, :])[:, :ROPE_D]
        kr_ref[rows, :] = kr
        if per_head_kv:
            ckv_b = ckv.astype(BF16)
            k_all = _dot(ckv_b, wuk_ref[...]).astype(BF16)
            v_all = _dot(ckv_b, wuv_ref[...]).astype(BF16)
            tail = jnp.concatenate([kr.astype(BF16),
                                    jnp.zeros((sub, HEAD_QK - NOPE - ROPE_D), BF16)], axis=1)
            for hd in range(MLA_HEADS):
                kc_ref[hd, rows, 0:NOPE] = k_all[:, hd * NOPE:(hd + 1) * NOPE]
                kc_ref[hd, rows, NOPE:] = tail
                v_ref[hd, rows, :] = v_all[:, hd * V_D:(hd + 1) * V_D]


def _dqkv(x, g, mod, layer, w_pad, q_norm, kv_norm, cos_t, sin_t, tm, rope_blocks, seq=None,
          w_uk_flat=None, w_uv_flat=None):
    m, d = x.shape
    n = w_pad.shape[1]
    rb = rope_blocks
    per_head_kv = w_uk_flat is not None
    in_specs = [
        pl.BlockSpec((tm, d), lambda i: (i, 0)),
        pl.BlockSpec((1, d), lambda i: (0, 0)),
        mod.spec(layer, 0),
        mod.spec(layer, 1),
        pl.BlockSpec((d, n), lambda i: (0, 0)),
        pl.BlockSpec((1, Q_LORA), lambda i: (0, 0)),
        pl.BlockSpec((1, KV_LORA), lambda i: (0, 0)),
        pl.BlockSpec((cos_t.shape[0] // rb, LANES), lambda i: (i % rb, 0)),
        pl.BlockSpec((cos_t.shape[0] // rb, LANES), lambda i: (i % rb, 0)),
    ]
    out_specs = [
        pl.BlockSpec((tm, Q_LORA), lambda i: (i, 0)),
        pl.BlockSpec((tm, KV_LORA), lambda i: (i, 0)),
        pl.BlockSpec((tm, ROPE_D), lambda i: (i, 0)),
    ]
    out_shape = [
        jax.ShapeDtypeStruct((m, Q_LORA), BF16),
        jax.ShapeDtypeStruct((m, KV_LORA), F32),
        jax.ShapeDtypeStruct((m, ROPE_D), F32),
    ]
    args = [x, g, mod.arr, mod.arr, w_pad, q_norm, kv_norm, cos_t, sin_t]
    if per_head_kv:
        tps = seq // tm
        nb = m // seq
        in_specs += [pl.BlockSpec(w_uk_flat.shape, lambda i: (0, 0)),
                     pl.BlockSpec(w_uv_flat.shape, lambda i: (0, 0))]
        out_specs += [
            pl.BlockSpec((None, MLA_HEADS, tm, HEAD_QK), lambda i: (i // tps, 0, i % tps, 0)),
            pl.BlockSpec((None, MLA_HEADS, tm, V_D), lambda i: (i // tps, 0, i % tps, 0)),
        ]
        out_shape += [
            jax.ShapeDtypeStruct((nb, MLA_HEADS, seq, HEAD_QK), BF16),
            jax.ShapeDtypeStruct((nb, MLA_HEADS, seq, V_D), BF16),
        ]
        args += [w_uk_flat, w_uv_flat]
    return pl.pallas_call(
        functools.partial(_dqkv_kernel, per_head_kv=per_head_kv),
        grid=(m // tm,),
        in_specs=in_specs,
        out_specs=out_specs,
        out_shape=out_shape,
        compiler_params=_cp(("parallel",)),
        name="mla_down_proj",
    )(*args)


def _q_kernel(cq_ref, wn_ref, wr_ref, cos_ref, sin_ref, *rest, absorb):
    cq = cq_ref[...]
    qn = _dot(cq, wn_ref[...])
    qr = _dot(cq, wr_ref[...])
    cos = jnp.concatenate([cos_ref[...]] * (MLA_HEADS * ROPE_D // LANES), axis=1)
    sin = jnp.concatenate([sin_ref[...]] * (MLA_HEADS * ROPE_D // LANES), axis=1)
    qr = _rope_pairs(qr, cos, sin).astype(BF16)
    if absorb:
        wuk_ref, ql_ref, qr_ref = rest
        for h in range(MLA_HEADS):
            ql_ref[h] = _dot(qn[:, h * NOPE:(h + 1) * NOPE].astype(BF16), wuk_ref[h]).astype(BF16)
            qr_ref[h] = qr[:, h * ROPE_D:(h + 1) * ROPE_D]
    else:
        (qc_ref,) = rest
        qn = qn.astype(BF16)
        zeros = jnp.zeros((cq.shape[0], HEAD_QK - NOPE - ROPE_D), BF16)
        for h in range(MLA_HEADS):
            qc_ref[h, :, 0:NOPE] = qn[:, h * NOPE:(h + 1) * NOPE]
            qc_ref[h, :, NOPE:] = jnp.concatenate([qr[:, h * ROPE_D:(h + 1) * ROPE_D], zeros], axis=1)


def _q_proj(cq, w_nope, w_rope, w_ukt, cos_t, sin_t, nb, seq, tm, rope_blocks):
    nt = seq // tm
    rb = rope_blocks
    absorb = w_ukt is not None
    in_specs = [
        pl.BlockSpec((tm, Q_LORA), lambda n, t: (n * nt + t, 0)),
        pl.BlockSpec(w_nope.shape, lambda n, t: (0, 0)),
        pl.BlockSpec(w_rope.shape, lambda n, t: (0, 0)),
        pl.BlockSpec((cos_t.shape[0] // rb, LANES), lambda n, t: (t % rb, 0)),
        pl.BlockSpec((cos_t.shape[0] // rb, LANES), lambda n, t: (t % rb, 0)),
    ]
    args = [cq, w_nope, w_rope, cos_t, sin_t]
    if absorb:
        in_specs.append(pl.BlockSpec(w_ukt.shape, lambda n, t: (0, 0, 0)))
        args.append(w_ukt)
        widths = (KV_LORA, ROPE_D)
    else:
        widths = (HEAD_QK,)
    return pl.pallas_call(
        functools.partial(_q_kernel, absorb=absorb),
        grid=(nb, nt),
        in_specs=in_specs,
        out_specs=[pl.BlockSpec((None, MLA_HEADS, tm, w), lambda n, t: (n, 0, t, 0)) for w in widths],
        out_shape=[jax.ShapeDtypeStruct((nb, MLA_HEADS, seq, w), BF16) for w in widths],
        compiler_params=_cp(("parallel", "parallel")),
        name="mla_q_proj",
    )(*args)


def _lane_tile(x, width):
    return x if width == LANES else jnp.concatenate([x] * (width // LANES), axis=1)


def _prompt_attn_kernel(q_ref, k_ref, v_ref, o_ref, m_scr, l_scr, acc_scr, *, tq):
    i = pl.program_id(1)
    m_scr[...] = jnp.full_like(m_scr, NEG)
    l_scr[...] = jnp.zeros_like(l_scr)
    acc_scr[...] = jnp.zeros_like(acc_scr)

    def block(j, masked):
        r0 = pl.multiple_of(j * tq, tq)
        if masked:
            causal = (lax.broadcasted_iota(jnp.int32, (tq, tq), 1)
                      <= lax.broadcasted_iota(jnp.int32, (tq, tq), 0))

        def head(h, carry):
            s = _dot_nt(q_ref[h], k_ref[h, pl.ds(r0, tq), :]) * (MLA_SCALE * LOG2E)
            if masked:
                s = jnp.where(causal, s, NEG)
            m_prev = m_scr[h]
            m_new = jnp.maximum(m_prev, jnp.max(s, axis=-1, keepdims=True))
            alpha = jnp.exp2(m_prev - m_new)
            p = jnp.exp2(s - _lane_tile(m_new, tq))
            l_scr[h] = alpha * l_scr[h] + jnp.sum(p, axis=-1, keepdims=True)
            acc_scr[h] = alpha * acc_scr[h] + _dot(p.astype(BF16), v_ref[h, pl.ds(r0, tq), :])
            m_scr[h] = m_new
            return carry

        lax.fori_loop(0, MLA_HEADS, head, 0, unroll=True)

    def body(j, carry):
        block(j, False)
        return carry

    lax.fori_loop(0, i, body, 0)
    block(i, True)
    for h in range(MLA_HEADS):
        o_ref[:, h * V_D:(h + 1) * V_D] = (acc_scr[h] / l_scr[h]).astype(BF16)


def _prompt_attn(q_cat, k_cat, v_heads, nb, seq, tq):
    nq = seq // tq
    return pl.pallas_call(
        functools.partial(_prompt_attn_kernel, tq=tq),
        grid=(nb, nq),
        in_specs=[
            pl.BlockSpec((None, MLA_HEADS, tq, HEAD_QK), lambda n, i: (n, 0, i, 0)),
            pl.BlockSpec((None, MLA_HEADS, seq, HEAD_QK), lambda n, i: (n, 0, 0, 0)),
            pl.BlockSpec((None, MLA_HEADS, seq, V_D), lambda n, i: (n, 0, 0, 0)),
        ],
        out_specs=pl.BlockSpec((tq, MLA_HEADS * V_D), lambda n, i: (n * nq + i, 0)),
        out_shape=jax.ShapeDtypeStruct((nb * seq, MLA_HEADS * V_D), BF16),
        scratch_shapes=[pltpu.VMEM((MLA_HEADS, tq, LANES), F32),
                        pltpu.VMEM((MLA_HEADS, tq, LANES), F32),
                        pltpu.VMEM((MLA_HEADS, tq, V_D), F32)],
        compiler_params=_cp(("parallel", "arbitrary")),
        name="mla_prompt_attn",
    )(q_cat, k_cat, v_heads)


def _sample_attn_kernel(pt_ref, ql_ref, qr_ref, cn_ref, krn_ref, ckv_hbm, krt_hbm, o_ref,
                        kbuf, rbuf, kb16, kt16, sems, *, layer_j, n_pages):
    b = pl.program_id(0)
    nb = pl.num_programs(0)
    slot = b % 2

    def page_copies(seq, slot_, p):
        page = pt_ref[seq, p]
        return (
            pltpu.make_async_copy(ckv_hbm.at[layer_j, page],
                                  kbuf.at[slot_, p * PAGE_SIZE:(p + 1) * PAGE_SIZE, :],
                                  sems.at[0, slot_]),
            pltpu.make_async_copy(krt_hbm.at[layer_j, page],
                                  rbuf.at[slot_, :, p * PAGE_SIZE:(p + 1) * PAGE_SIZE],
                                  sems.at[1, slot_]),
        )

    def start_all(seq, slot_):
        for p in range(n_pages):
            for cp in page_copies(seq, slot_, p):
                cp.start()

    @pl.when(b == 0)
    def _():
        start_all(0, 0)

    @pl.when(b + 1 < nb)
    def _():
        start_all(b + 1, 1 - slot)

    for p in range(n_pages):
        for cp in page_copies(b, slot, p):
            cp.wait()

    ql = ql_ref[...]
    qr = qr_ref[...]
    past = n_pages * PAGE_SIZE
    tc = min(past, 1024)
    for c in range(past // tc):
        kb = kbuf[slot, c * tc:(c + 1) * tc, :].astype(BF16)
        kb16[c * tc:(c + 1) * tc, :] = kb
        kt16[:, c * tc:(c + 1) * tc] = kb.T
    s = (_dot(ql, kt16[...]) + _dot(qr, rbuf[slot].astype(BF16))) * MLA_SCALE
    cn = cn_ref[...]
    s_new = (jnp.sum(ql.astype(F32) * cn, axis=-1, keepdims=True)
             + jnp.sum(qr.astype(F32) * krn_ref[...], axis=-1, keepdims=True)) * MLA_SCALE
    m = jnp.maximum(jnp.max(s, axis=-1, keepdims=True), s_new)
    p = jnp.exp(s - m)
    p_new = jnp.exp(s_new - m)
    l = jnp.sum(p, axis=-1, keepdims=True) + p_new
    acc = _dot(p.astype(BF16), kb16[...]) + p_new * cn
    o_ref[...] = (acc / l).astype(BF16)


def _sample_attn(page_table, q_lat, q_rope, ckv_new, kr_new, cache_ckv, cache_krope_t, layer_j):
    nb, n_pages = page_table.shape
    past = n_pages * PAGE_SIZE
    grid_spec = pltpu.PrefetchScalarGridSpec(
        num_scalar_prefetch=1,
        grid=(nb,),
        in_specs=[
            pl.BlockSpec((None, MLA_HEADS, KV_LORA), lambda b, pt: (b, 0, 0)),
            pl.BlockSpec((None, MLA_HEADS, ROPE_D), lambda b, pt: (b, 0, 0)),
            pl.BlockSpec((None, 1, KV_LORA), lambda b, pt: (b, 0, 0)),
            pl.BlockSpec((None, 1, ROPE_D), lambda b, pt: (b, 0, 0)),
            pl.BlockSpec(memory_space=pl.ANY),
            pl.BlockSpec(memory_space=pl.ANY),
        ],
        out_specs=pl.BlockSpec((None, MLA_HEADS, KV_LORA), lambda b, pt: (b, 0, 0)),
        scratch_shapes=[
            pltpu.VMEM((2, past, KV_LORA), F32),
            pltpu.VMEM((2, ROPE_D, past), F32),
            pltpu.VMEM((past, KV_LORA), BF16),
            pltpu.VMEM((KV_LORA, past), BF16),
            pltpu.SemaphoreType.DMA((2, 2)),
        ],
    )
    return pl.pallas_call(
        functools.partial(_sample_attn_kernel, layer_j=layer_j, n_pages=n_pages),
        grid_spec=grid_spec,
        out_shape=jax.ShapeDtypeStruct((nb, MLA_HEADS, KV_LORA), BF16),
        compiler_params=_cp(("arbitrary",)),
        name="mla_sample_attn",
    )(page_table, q_lat, q_rope, ckv_new.reshape(nb, 1, KV_LORA), kr_new.reshape(nb, 1, ROPE_D),
      cache_ckv, cache_krope_t)


def _tile(m, pref):
    t = min(m, pref)
    assert m % t == 0, (m, t)
    return t


def kernel(x_prompt, x_sample, c_prompt, c_sample, cache_conv, state_hgrn, cache_ckv, cache_krope,
           page_table, w_ada, b_ada, norm_mix, norm_ffn, w_in_even, conv_w, hg_lb_logits, hg_gnorm,
           w_out_even, w_dqkv, q_norm, w_uq, kv_norm, w_uk, w_uv, w_o, w_gu, w_down, norm_final):
    bp, tp, d = x_prompt.shape
    bs, ts, _ = x_sample.shape
    assert ts == 1 and d == D_MODEL
    depth = w_ada.shape[0]
    n_pages = page_table.shape[1]
    past_len = n_pages * PAGE_SIZE
    mp, ms = bp * tp, bs

    tm_p = _tile(tp, 512)
    tm_s = ms
    tt = _tile(tp, 256)
    tq = _tile(tp, 512)
    tb = _tile(bs, 16)
    ns = 1
    krope_t = jnp.swapaxes(cache_krope, 2, 3)
    fc = 256

    mods = _ada_all(jnp.concatenate([c_prompt, c_sample], axis=0), w_ada, b_ada)
    mod_p = _Mod(mods[:, :bp].reshape(depth, bp, 1, N_MOD * d), tp // tm_p)
    mod_s = _Mod(mods[:, bp:].reshape(depth, 1, bs, N_MOD * d), 1)

    cos_p, sin_p = _rope_table(tp, 0, 1, _tile(tp, 512))
    cos_s, sin_s = _rope_table(8, past_len, 0, 8)
    cos_s = jnp.broadcast_to(cos_s[:1], (ms, LANES))
    sin_s = jnp.broadcast_to(sin_s[:1], (ms, LANES))

    w_in_b = w_in_even.astype(BF16)
    w_out_b = w_out_even.astype(BF16)
    w_gu_b = w_gu.astype(BF16)
    w_down_b = w_down.astype(BF16)
    w_o_b = w_o.astype(BF16)
    w_uvh_b = jnp.transpose(w_uv, (0, 2, 1, 3)).astype(BF16)

    xp = x_prompt.reshape(mp, d)
    xs = x_sample.reshape(ms, d)
    outs ={k: [] for k in ("conv_p", "hg_p", "ckv_p", "kr_p", "conv_s", "hg_s", "ckv_s", "kr_s")}

    for l in range(depth):
        j = l // 2
        g_mix = norm_mix[l].reshape(1, d)
        g_ffn = norm_ffn[l].reshape(1, d)
        if l % 2 == 0:
            gn = hg_gnorm[j].reshape(1, HG_DK)
            zp = _mod_matmul(xp, g_mix, mod_p, l, w_in_b, j, tm_p)
            zs = _mod_matmul(xs, g_mix, mod_s, l, w_in_b, j, tm_s)
            abp, cvp, hgp = _prompt_mixer(zp.reshape(bp, tp, EVEN_IN), conv_w[j], hg_lb_logits, gn,
                                          j, bp, tp, tt, ns)
            abp = abp.reshape(mp, d)
            abs_, cvs, hgs = _sample_mixer(zs, cache_conv, state_hgrn, conv_w[j], hg_lb_logits,
                                           gn, j, tb)
            mix_p, mix_s = (abp, w_out_b), (abs_, w_out_b)
            mix_w_specs_p = mix_w_specs = [_resident(w_out_b, j)]
            outs["conv_p"].append(cvp)
            outs["hg_p"].append(hgp)
            outs["conv_s"].append(cvs)
            outs["hg_s"].append(hgs)
        else:
            w_d = jnp.pad(w_dqkv[j], ((0, 0), (0, ROPE_D))).astype(BF16)
            wq = w_uq[j].reshape(Q_LORA, MLA_HEADS, NOPE + ROPE_D)
            w_nope = wq[:, :, :NOPE].reshape(Q_LORA, MLA_HEADS * NOPE).astype(BF16)
            w_rope = wq[:, :, NOPE:].reshape(Q_LORA, MLA_HEADS * ROPE_D).astype(BF16)
            w_ukt = jnp.transpose(w_uk[j], (1, 2, 0)).astype(BF16)
            qn = q_norm[j].reshape(1, Q_LORA)
            kvn = kv_norm[j].reshape(1, KV_LORA)

            w_uk_flat = w_uk[j].reshape(KV_LORA, MLA_HEADS * NOPE).astype(BF16)
            w_uv_flat = w_uv[j].reshape(KV_LORA, MLA_HEADS * V_D).astype(BF16)
            cqp, ckvp, krp, kcp, vhp = _dqkv(xp, g_mix, mod_p, l, w_d, qn, kvn, cos_p, sin_p, tm_p,
                                             tp // tm_p, tp, w_uk_flat, w_uv_flat)
            cqs, ckvs, krs = _dqkv(xs, g_mix, mod_s, l, w_d, qn, kvn, cos_s, sin_s, tm_s, 1)
            (qcp,) = _q_proj(cqp, w_nope, w_rope, None, cos_p, sin_p, bp, tp, tm_p, tp // tm_p)
            qls, qrs = _q_proj(cqs, w_nope, w_rope, w_ukt, cos_s, sin_s, 1, ms, tm_s, 1)
            ctxp = _prompt_attn(qcp, kcp, vhp, bp, tp, tq)
            ctxs = _sample_attn(page_table,
                                jnp.transpose(qls[0], (1, 0, 2)), jnp.transpose(qrs[0], (1, 0, 2)),
                                ckvs, krs, cache_ckv, krope_t, j)
            mix_p = (ctxp, w_o_b)
            mix_s = (ctxs.reshape(ms, MLA_HEADS * KV_LORA), w_uvh_b, w_o_b)
            mix_w_specs_p = [_resident(w_o_b, j)]
            mix_w_specs = [_resident(w_uvh_b, j), _resident(w_o_b, j)]
            outs["ckv_p"].append(ckvp.reshape(bp, tp, KV_LORA))
            outs["kr_p"].append(krp.reshape(bp, tp, ROPE_D))
            outs["ckv_s"].append(ckvs.reshape(bs, ts, KV_LORA))
            outs["kr_s"].append(krs.reshape(bs, ts, ROPE_D))

        last = l == depth - 1
        gfin = norm_final.reshape(1, d)
        def row_spec(a, tm):
            return pl.BlockSpec((tm, a.shape[1]), lambda i: (i, 0))

        xp = _ffn(mix_p, [row_spec(mix_p[0], tm_p)] + mix_w_specs_p, xp, g_ffn, mod_p, l,
                  w_gu_b, w_down_b, gfin, last, tm_p, fc)
        xs = _ffn(mix_s, [row_spec(mix_s[0], tm_s)] + mix_w_specs, xs, g_ffn, mod_s, l,
                  w_gu_b, w_down_b, gfin, last, tm_s, fc)

    return (xp.reshape(bp, tp, d), xs.reshape(bs, ts, d),
            jnp.stack(outs["conv_p"]), jnp.stack(outs["hg_p"]),
            jnp.stack(outs["ckv_p"]), jnp.stack(outs["kr_p"]),
            jnp.stack(outs["conv_s"]), jnp.stack(outs["hg_s"]),
            jnp.stack(outs["ckv_s"]), jnp.stack(outs["kr_s"]))
```

```python
import functools
import math

import jax
import jax.numpy as jnp
from jax import lax
from jax.experimental import pallas as pl
from jax.experimental.pallas import tpu as pltpu

F32 = jnp.float32
BF16 = jnp.bfloat16

D_MODEL = 1024
N_MOD = 6
EPS = 1e-6
NEG = -1e30
PAGE_SIZE = 128
CONV_CH = D_MODEL // 2
CONV_W = 3
HG_WIDTH = D_MODEL // 2
HG_DK = 128
HG_HEADS = HG_WIDTH // HG_DK
EVEN_IN = 3 * CONV_CH + 4 * HG_WIDTH
MLA_HEADS = 8
NOPE = 128
ROPE_D = 64
V_D = 128
Q_LORA = 384
KV_LORA = 256
ROPE_THETA = 10000.0
MLA_SCALE = (NOPE + ROPE_D) ** -0.5
HEAD_QK = 192
LANES = 128
SUBLANES = 8
LOG2E = 1.0 / math.log(2.0)

Z_B, Z_C, Z_X = 0, CONV_CH, 2 * CONV_CH
Z_Q = 3 * CONV_CH
Z_F = Z_Q + HG_WIDTH
Z_I = Z_F + HG_WIDTH
Z_G = Z_I + HG_WIDTH

HG_SUB = 16
HG_UNROLL = 2
VMEM_LIMIT = 56 * 1024 * 1024


def _cp(sem, vmem=VMEM_LIMIT):
    return pltpu.CompilerParams(dimension_semantics=sem, vmem_limit_bytes=vmem)


def _silu(x):
    return x * jax.nn.sigmoid(x)


def _rmsnorm(x, g):
    return x * lax.rsqrt(jnp.mean(x * x, axis=-1, keepdims=True) + EPS) * g


def _dot(a, b):
    return jnp.dot(a, b, preferred_element_type=F32)


def _dot_nt(a, b):
    return lax.dot_general(a, b, (((1,), (1,)), ((), ())), preferred_element_type=F32)


def _dot_tn(a, b):
    return lax.dot_general(a, b, (((0,), (0,)), ((), ())), preferred_element_type=F32)


def _ada_kernel(c_ref, w_ref, b_ref, o_ref):
    a = _silu(c_ref[...]).astype(BF16)
    o_ref[...] = _dot(a, w_ref[...].astype(BF16)) + b_ref[...]


def _ada_all(c_all, w_ada, b_ada, tn=1536):
    depth, d, n6 = w_ada.shape
    rows = c_all.shape[0]
    return pl.pallas_call(
        _ada_kernel,
        grid=(depth, n6 // tn),
        in_specs=[
            pl.BlockSpec((rows, d), lambda l, j: (0, 0)),
            pl.BlockSpec((None, d, tn), lambda l, j: (l, 0, j)),
            pl.BlockSpec((None, 1, tn), lambda l, j: (l, 0, j)),
        ],
        out_specs=pl.BlockSpec((None, rows, tn), lambda l, j: (l, 0, j)),
        out_shape=jax.ShapeDtypeStruct((depth, rows, n6), F32),
        compiler_params=_cp(("parallel", "parallel")),
        name="adaln_mod",
    )(c_all, w_ada, b_ada.reshape(depth, 1, n6))


class _Mod:
    def __init__(self, arr, tps):
        self.arr = arr
        self.tps = tps
        self.r = arr.shape[2]

    def spec(self, layer, col, width=D_MODEL, ncol=None):
        tps = self.tps
        per = D_MODEL // width
        if ncol is None:
            return pl.BlockSpec((None, None, self.r, width),
                                lambda i, *_: (layer, i // tps, 0, col * per))
        return pl.BlockSpec((None, None, self.r, width),
                            lambda i, j, *_: (layer, i // tps, 0, col * per + j))


def _resident(stacked, layer):
    shape = stacked.shape[1:]
    return pl.BlockSpec((None,) + shape, lambda *_: (layer,) + (0,) * len(shape),
                        pipeline_mode=pl.Buffered(1))


def _modmm_kernel(x_ref, g_ref, sh_ref, sc_ref, w_ref, o_ref):
    h = _rmsnorm(x_ref[...], g_ref[...]) * (1.0 + sc_ref[...]) + sh_ref[...]
    o_ref[...] = _dot(h.astype(BF16), w_ref[...])


def _mod_matmul(x, g, mod, layer, w_all, j, tm):
    m, d = x.shape
    n = w_all.shape[2]
    return pl.pallas_call(
        _modmm_kernel,
        grid=(m // tm,),
        in_specs=[
            pl.BlockSpec((tm, d), lambda i: (i, 0)),
            pl.BlockSpec((1, d), lambda i: (0, 0)),
            mod.spec(layer, 0),
            mod.spec(layer, 1),
            _resident(w_all, j),
        ],
        out_specs=pl.BlockSpec((tm, n), lambda i: (i, 0)),
        out_shape=jax.ShapeDtypeStruct((m, n), F32),
        compiler_params=_cp(("parallel",)),
        name="mod_matmul",
    )(x, g, mod.arr, mod.arr, w_all)


def _ffn_kernel(*refs, final_norm, dff, fc, n_mix):
    mix = refs[:n_mix]
    (x_ref, gmix_ref, g_ref, sh_ref, sc_ref, gate_ref, wgu_ref, wd_ref, gf_ref, o_ref,
     a_scr) = refs[n_mix:]
    if n_mix == 2:
        a_ref, wout_ref = mix
        y_mix = _dot(a_ref[...], wout_ref[...])
    else:
        ctx_ref, wuv_ref, wo_ref = mix
        parts = [_dot(ctx_ref[:, h * KV_LORA:(h + 1) * KV_LORA], wuv_ref[h]).astype(BF16)
                 for h in range(MLA_HEADS)]
        y_mix = _dot(jnp.concatenate(parts, axis=1), wo_ref[...])
    x = x_ref[...] + gmix_ref[...] * y_mix
    h = (_rmsnorm(x, g_ref[...]) * (1.0 + sc_ref[...]) + sh_ref[...]).astype(BF16)
    for c in range(dff // fc):
        gg = _dot(h, wgu_ref[:, c * fc:(c + 1) * fc])
        uu = _dot(h, wgu_ref[:, dff + c * fc:dff + (c + 1) * fc])
        a_scr[:, c * fc:(c + 1) * fc] = (_silu(gg) * uu).astype(BF16)
    y = x + gate_ref[...] * _dot(a_scr[...], wd_ref[...])
    if final_norm:
        y = _rmsnorm(y, gf_ref[...])
    o_ref[...] = y


def _ffn(mix, mix_specs, x, g, mod, layer, w_gu, w_down, g_final, final_norm, tm, fc):
    m, d = x.shape
    dff = w_down.shape[1]
    return pl.pallas_call(
        functools.partial(_ffn_kernel, final_norm=final_norm, dff=dff, fc=fc, n_mix=len(mix)),
        grid=(m // tm,),
        in_specs=list(mix_specs) + [
            pl.BlockSpec((tm, d), lambda i: (i, 0)),
            mod.spec(layer, 2),
            pl.BlockSpec((1, d), lambda i: (0, 0)),
            mod.spec(layer, 3),
            mod.spec(layer, 4),
            mod.spec(layer, 5),
            _resident(w_gu, layer),
            _resident(w_down, layer),
            pl.BlockSpec((1, d), lambda i: (0, 0)),
        ],
        out_specs=pl.BlockSpec((tm, d), lambda i: (i, 0)),
        out_shape=jax.ShapeDtypeStruct((m, d), F32),
        scratch_shapes=[pltpu.VMEM((tm, dff), BF16)],
        compiler_params=_cp(("parallel",)),
        name="mix_out_ffn",
    )(*mix, x, mod.arr, g, mod.arr, mod.arr, mod.arr, w_gu, w_down, g_final)


def _hg_lower_bound(lbl_ref, j):
    logits = lbl_ref[...]
    e = jnp.exp(logits - jnp.max(logits, axis=0, keepdims=True))
    den = jnp.sum(e, axis=0, keepdims=True)
    lb = jnp.zeros_like(den)
    for i in range(j):
        lb = lb + e[i:i + 1, :] / den
    return lb


def _cumsum_rows(x):
    rows = x.shape[0]
    idx = lax.broadcasted_iota(jnp.int32, x.shape, 0)
    d = 1
    while d < rows:
        x = x + jnp.where(idx >= d, pltpu.roll(x, d, 0), 0.0)
        d *= 2
    return x


def _hgrn_gates(qp, fp, lb):
    logf = jnp.log(lb + (1.0 - lb) * jax.nn.sigmoid(fp))
    kk = (1.0 - lb) * jax.nn.sigmoid(-fp)
    q = _silu(qp) * (HG_DK ** -0.5)
    return q, kk, logf


def _prompt_mixer_kernel(z_ref, cw_ref, lbl_ref, gn_ref, ab_ref, conv_ref, s_ref, ubuf, st_scr,
                         *, layer_j, tt, ns):
    t = pl.program_id(1)
    nt = pl.num_programs(1)

    @pl.when(t == 0)
    def _():
        ubuf[:, 0:8, :] = jnp.zeros((ns, 8, CONV_CH), F32)
        st_scr[...] = jnp.zeros_like(st_scr)

    for n in range(ns):
        u = z_ref[n, :, Z_C:Z_C + CONV_CH] * z_ref[n, :, Z_X:Z_X + CONV_CH]
        ubuf[n, 8:8 + tt, :] = u
        y = (cw_ref[0:1, :] * ubuf[n, 6:6 + tt, :] + cw_ref[1:2, :] * ubuf[n, 7:7 + tt, :]
             + cw_ref[2:3, :] * ubuf[n, 8:8 + tt, :])
        ab_ref[n, :, 0:CONV_CH] = (z_ref[n, :, Z_B:Z_B + CONV_CH] * y).astype(BF16)
        last2 = ubuf[n, tt + 6:tt + 8, :]
        ubuf[n, 6:8, :] = last2
        conv_ref[n] = last2

    lb_all = _hg_lower_bound(lbl_ref, layer_j)
    gn = gn_ref[...]
    ell = HG_SUB
    row8 = lax.broadcasted_iota(jnp.int32, (SUBLANES, HG_DK), 0)

    def chunk(c, carry):
        r0 = pl.multiple_of(c * ell, ell)
        for n in range(ns):
            for h in range(HG_HEADS):
                lo = h * HG_DK
                lb = lb_all[:, lo:lo + HG_DK]
                qp = z_ref[n, pl.ds(r0, ell), Z_Q + lo:Z_Q + lo + HG_DK]
                fp = z_ref[n, pl.ds(r0, ell), Z_F + lo:Z_F + lo + HG_DK]
                v = z_ref[n, pl.ds(r0, ell), Z_I + lo:Z_I + lo + HG_DK]
                gp = z_ref[n, pl.ds(r0, ell), Z_G + lo:Z_G + lo + HG_DK]
                q, kk, logf = _hgrn_gates(qp, fp, lb)
                gc = _cumsum_rows(logf)
                gl = gc[ell - 1:ell, :]
                st = st_scr[n, h]
                o = _dot_nt((q * jnp.exp(gc)).astype(BF16), st.astype(BF16))
                o_grp = [o[r:r + SUBLANES, :] for r in range(0, ell, SUBLANES)]
                gc2 = gc * LOG2E
                for s in range(ell):
                    for gi, r in enumerate(range(0, ell, SUBLANES)):
                        if r + SUBLANES <= s:
                            continue
                        d = gc2[r:r + SUBLANES, :] - gc2[s:s + 1, :]
                        if r <= s:
                            d = jnp.where(row8 >= s - r, d, NEG)
                        w = q[r:r + SUBLANES, :] * kk[s:s + 1, :] * jnp.exp2(d)
                        o_grp[gi] = (o_grp[gi]
                                     + jnp.sum(w, axis=-1, keepdims=True) * v[s:s + 1, :])
                o = jnp.concatenate(o_grp, axis=0)
                kd = kk * jnp.exp(gl - gc)
                st_scr[n, h] = st * jnp.exp(gl) + _dot_tn(v.astype(BF16), kd.astype(BF16))
                b = _rmsnorm(o, gn) * _silu(gp)
                ab_ref[n, pl.ds(r0, ell), CONV_CH + lo:CONV_CH + lo + HG_DK] = b.astype(BF16)
        return carry

    lax.fori_loop(0, tt // ell, chunk, 0, unroll=HG_UNROLL)

    @pl.when(t == nt - 1)
    def _():
        for n in range(ns):
            for h in range(HG_HEADS):
                s_ref[n, h] = st_scr[n, h].T


def _prompt_mixer(z, conv_w_j, lb_logits, gnorm_j, layer_j, nb, seq, tt, ns):
    nt = seq // tt
    return pl.pallas_call(
        functools.partial(_prompt_mixer_kernel, layer_j=layer_j, tt=tt, ns=ns),
        grid=(nb // ns, nt),
        in_specs=[
            pl.BlockSpec((ns, tt, EVEN_IN), lambda g, t: (g, t, 0)),
            pl.BlockSpec((CONV_W, CONV_CH), lambda g, t: (0, 0)),
            pl.BlockSpec(lb_logits.shape, lambda g, t: (0, 0)),
            pl.BlockSpec((1, HG_DK), lambda g, t: (0, 0)),
        ],
        out_specs=[
            pl.BlockSpec((ns, tt, D_MODEL), lambda g, t: (g, t, 0)),
            pl.BlockSpec((ns, CONV_W - 1, CONV_CH), lambda g, t: (g, 0, 0)),
            pl.BlockSpec((ns, HG_HEADS, HG_DK, HG_DK), lambda g, t: (g, 0, 0, 0)),
        ],
        out_shape=[
            jax.ShapeDtypeStruct((nb, seq, D_MODEL), BF16),
            jax.ShapeDtypeStruct((nb, CONV_W - 1, CONV_CH), F32),
            jax.ShapeDtypeStruct((nb, HG_HEADS, HG_DK, HG_DK), F32),
        ],
        scratch_shapes=[pltpu.VMEM((ns, tt + 8, CONV_CH), F32),
                        pltpu.VMEM((ns, HG_HEADS, HG_DK, HG_DK), F32)],
        compiler_params=_cp(("parallel", "arbitrary")),
        name="prompt_conv_hgrn",
    )(z, conv_w_j, lb_logits, gnorm_j)


def _column(row_vec, eye):
    return jnp.sum(jnp.where(eye, row_vec, 0.0), axis=1, keepdims=True)


def _sample_mixer_kernel(z_ref, cb_ref, s0_ref, cw_ref, lbl_ref, gn_ref, ab_ref, conv_ref, s_ref,
                         b_scr, *, layer_j, tb):
    u = z_ref[:, Z_C:Z_C + CONV_CH] * z_ref[:, Z_X:Z_X + CONV_CH]
    b0 = cb_ref[:, 0, :]
    b1 = cb_ref[:, 1, :]
    y = cw_ref[0:1, :] * b0 + cw_ref[1:2, :] * b1 + cw_ref[2:3, :] * u
    ab_ref[:, 0:CONV_CH] = (z_ref[:, Z_B:Z_B + CONV_CH] * y).astype(BF16)
    conv_ref[:, 0, :] = b1
    conv_ref[:, 1, :] = u

    lb_all = _hg_lower_bound(lbl_ref, layer_j)
    gn = gn_ref[...]
    eye = (lax.broadcasted_iota(jnp.int32, (HG_DK, HG_DK), 0)
           == lax.broadcasted_iota(jnp.int32, (HG_DK, HG_DK), 1))

    q_all, kk_all, logf_all = _hgrn_gates(z_ref[:, Z_Q:Z_Q + HG_WIDTH], z_ref[:, Z_F:Z_F + HG_WIDTH],
                                          lb_all)
    ef_all = jnp.exp(logf_all)
    v_all = z_ref[:, Z_I:Z_I + HG_WIDTH]
    for n in range(tb):
        for h in range(HG_HEADS):
            lo = h * HG_DK
            row = lambda a: a[n:n + 1, lo:lo + HG_DK]
            s_new = (_column(row(ef_all), eye) * s0_ref[n, h]
                     + _column(row(kk_all), eye) * row(v_all))
            s_ref[n, h] = s_new
            b_scr[n:n + 1, lo:lo + HG_DK] = jnp.sum(_column(row(q_all), eye) * s_new, axis=0,
                                                    keepdims=True)
    for h in range(HG_HEADS):
        lo = h * HG_DK
        b = _rmsnorm(b_scr[:, lo:lo + HG_DK], gn) * _silu(z_ref[:, Z_G + lo:Z_G + lo + HG_DK])
        ab_ref[:, CONV_CH + lo:CONV_CH + lo + HG_DK] = b.astype(BF16)


def _sample_mixer(z, conv_buf, s0, conv_w_j, lb_logits, gnorm_j, layer_j, tb):
    nb = z.shape[0]
    return pl.pallas_call(
        functools.partial(_sample_mixer_kernel, layer_j=layer_j, tb=tb),
        grid=(nb // tb,),
        in_specs=[
            pl.BlockSpec((tb, EVEN_IN), lambda i: (i, 0)),
            pl.BlockSpec((None, tb, CONV_W - 1, CONV_CH), lambda i: (layer_j, i, 0, 0)),
            pl.BlockSpec((None, tb, HG_HEADS, HG_DK, HG_DK), lambda i: (layer_j, i, 0, 0, 0)),
            pl.BlockSpec((CONV_W, CONV_CH), lambda i: (0, 0)),
            pl.BlockSpec(lb_logits.shape, lambda i: (0, 0)),
            pl.BlockSpec((1, HG_DK), lambda i: (0, 0)),
        ],
        out_specs=[
            pl.BlockSpec((tb, D_MODEL), lambda i: (i, 0)),
            pl.BlockSpec((tb, CONV_W - 1, CONV_CH), lambda i: (i, 0, 0)),
            pl.BlockSpec((tb, HG_HEADS, HG_DK, HG_DK), lambda i: (i, 0, 0, 0)),
        ],
        out_shape=[
            jax.ShapeDtypeStruct((nb, D_MODEL), BF16),
            jax.ShapeDtypeStruct((nb, CONV_W - 1, CONV_CH), F32),
            jax.ShapeDtypeStruct((nb, HG_HEADS, HG_DK, HG_DK), F32),
        ],
        scratch_shapes=[pltpu.VMEM((tb, HG_WIDTH), F32)],
        compiler_params=_cp(("parallel",)),
        name="sample_conv_hgrn",
    )(z, conv_buf, s0, conv_w_j, lb_logits, gnorm_j)


def _rope_table_kernel(cos_ref, sin_ref, *, tr, pos0, step):
    lane = lax.broadcasted_iota(jnp.int32, (tr, LANES), 1)
    rowi = lax.broadcasted_iota(jnp.int32, (tr, LANES), 0)
    half = ROPE_D // 2
    fi = (lane % half).astype(F32)
    inv = jnp.exp(fi * (-math.log(ROPE_THETA) / half))
    pos = (pos0 + step * (pl.program_id(0) * tr + rowi)).astype(F32)
    ang = pos * inv
    sign = jnp.where((lane % ROPE_D) < half, -1.0, 1.0)
    cos_ref[...] = jnp.cos(ang)
    sin_ref[...] = jnp.sin(ang) * sign


def _rope_table(rows, pos0, step, tr):
    return pl.pallas_call(
        functools.partial(_rope_table_kernel, tr=tr, pos0=pos0, step=step),
        grid=(rows // tr,),
        out_specs=[pl.BlockSpec((tr, LANES), lambda i: (i, 0))] * 2,
        out_shape=[jax.ShapeDtypeStruct((rows, LANES), F32)] * 2,
        compiler_params=_cp(("parallel",)),
        name="rope_table",
    )()


def _rope_pairs(g, cos, sin_signed):
    lane = lax.broadcasted_iota(jnp.int32, g.shape, 1)
    half = ROPE_D // 2
    n = g.shape[1]
    rot = jnp.where((lane % ROPE_D) < half, pltpu.roll(g, n - half, 1), pltpu.roll(g, half, 1))
    return g * cos + rot * sin_signed


def _dqkv_kernel(x_ref, g_ref, sh_ref, sc_ref, w_ref, qn_ref, kvn_ref, cos_ref, sin_ref, *rest,
                 per_head_kv):
    if per_head_kv:
        wuk_ref, wuv_ref, cq_ref, ckv_ref, kr_ref, kc_ref, v_ref = rest
    else:
        cq_ref, ckv_ref, kr_ref = rest
    tm = x_ref.shape[0]
    sub = min(tm, 256)
    for r in range(0, tm, sub):
        rows = slice(r, r + sub)
        sc = sc_ref[...] if sc_ref.shape[0] == 1 else sc_ref[rows, :]
        sh = sh_ref[...] if sh_ref.shape[0] == 1 else sh_ref[rows, :]
        h = _rmsnorm(x_ref[rows, :], g_ref[...]) * (1.0 + sc) + sh
        d = _dot(h.astype(BF16), w_ref[...])
        cq_ref[rows, :] = _rmsnorm(d[:, :Q_LORA], qn_ref[...]).astype(BF16)
        ckv = _rmsnorm(d[:, Q_LORA:Q_LORA + KV_LORA], kvn_ref[...])
        ckv_ref[rows, :] = ckv
        kr = _rope_pairs(d[:, Q_LORA + KV_LORA:], cos_ref[rows, :], sin_ref[rows, :])[:, :ROPE_D]
        kr_ref[rows, :] = kr
        if per_head_kv:
            ckv_b = ckv.astype(BF16)
            k_all = _dot(ckv_b, wuk_ref[...]).astype(BF16)
            v_all = _dot(ckv_b, wuv_ref[...]).astype(BF16)
            tail = kr.astype(BF16)
            for hd in range(MLA_HEADS):
                kc_ref[hd, rows, 0:NOPE] = k_all[:, hd * NOPE:(hd + 1) * NOPE]
                kc_ref[hd, rows, NOPE:] = tail
                v_ref[hd, rows, :] = v_all[:, hd * V_D:(hd + 1) * V_D]


def _dqkv(x, g, mod, layer, w_pad, q_norm, kv_norm, cos_t, sin_t, tm, rope_blocks, seq=None,
          w_uk_flat=None, w_uv_flat=None):
    m, d = x.shape
    n = w_pad.shape[1]
    rb = rope_blocks
    per_head_kv = w_uk_flat is not None
    in_specs = [
        pl.BlockSpec((tm, d), lambda i: (i, 0)),
        pl.BlockSpec((1, d), lambda i: (0, 0)),
        mod.spec(layer, 0),
        mod.spec(layer, 1),
        pl.BlockSpec((d, n), lambda i: (0, 0)),
        pl.BlockSpec((1, Q_LORA), lambda i: (0, 0)),
        pl.BlockSpec((1, KV_LORA), lambda i: (0, 0)),
        pl.BlockSpec((cos_t.shape[0] // rb, LANES), lambda i: (i % rb, 0)),
        pl.BlockSpec((cos_t.shape[0] // rb, LANES), lambda i: (i % rb, 0)),
    ]
    out_specs = [
        pl.BlockSpec((tm, Q_LORA), lambda i: (i, 0)),
        pl.BlockSpec((tm, KV_LORA), lambda i: (i, 0)),
        pl.BlockSpec((tm, ROPE_D), lambda i: (i, 0)),
    ]
    out_shape = [
        jax.ShapeDtypeStruct((m, Q_LORA), BF16),
        jax.ShapeDtypeStruct((m, KV_LORA), F32),
        jax.ShapeDtypeStruct((m, ROPE_D), F32),
    ]
    args = [x, g, mod.arr, mod.arr, w_pad, q_norm, kv_norm, cos_t, sin_t]
    if per_head_kv:
        tps = seq // tm
        nb = m // seq
        in_specs += [pl.BlockSpec(w_uk_flat.shape, lambda i: (0, 0)),
                     pl.BlockSpec(w_uv_flat.shape, lambda i: (0, 0))]
        out_specs += [
            pl.BlockSpec((None, MLA_HEADS, tm, HEAD_QK), lambda i: (i // tps, 0, i % tps, 0)),
            pl.BlockSpec((None, MLA_HEADS, tm, V_D), lambda i: (i // tps, 0, i % tps, 0)),
        ]
        out_shape += [
            jax.ShapeDtypeStruct((nb, MLA_HEADS, seq, HEAD_QK), BF16),
            jax.ShapeDtypeStruct((nb, MLA_HEADS, seq, V_D), BF16),
        ]
        args += [w_uk_flat, w_uv_flat]
    return pl.pallas_call(
        functools.partial(_dqkv_kernel, per_head_kv=per_head_kv),
        grid=(m // tm,),
        in_specs=in_specs,
        out_specs=out_specs,
        out_shape=out_shape,
        compiler_params=_cp(("parallel",)),
        name="mla_down_proj",
    )(*args)


def _q_kernel(cq_ref, wn_ref, wr_ref, cos_ref, sin_ref, *rest, absorb):
    cq = cq_ref[...]
    qn = _dot(cq, wn_ref[...])
    qr = _dot(cq, wr_ref[...])
    cos = jnp.concatenate([cos_ref[...]] * (MLA_HEADS * ROPE_D // LANES), axis=1)
    sin = jnp.concatenate([sin_ref[...]] * (MLA_HEADS * ROPE_D // LANES), axis=1)
    qr = _rope_pairs(qr, cos, sin).astype(BF16)
    if absorb:
        wuk_ref, ql_ref, qr_ref = rest
        for h in range(MLA_HEADS):
            ql_ref[h] = _dot(qn[:, h * NOPE:(h + 1) * NOPE].astype(BF16), wuk_ref[h]).astype(BF16)
            qr_ref[h] = qr[:, h * ROPE_D:(h + 1) * ROPE_D]
    else:
        (qc_ref,) = rest
        qn = qn.astype(BF16)
        for h in range(MLA_HEADS):
            qc_ref[h, :, 0:NOPE] = qn[:, h * NOPE:(h + 1) * NOPE]
            qc_ref[h, :, NOPE:] = qr[:, h * ROPE_D:(h + 1) * ROPE_D]


def _q_proj(cq, w_nope, w_rope, w_ukt, cos_t, sin_t, nb, seq, tm, rope_blocks):
    nt = seq // tm
    rb = rope_blocks
    absorb = w_ukt is not None
    in_specs = [
        pl.BlockSpec((tm, Q_LORA), lambda n, t: (n * nt + t, 0)),
        pl.BlockSpec(w_nope.shape, lambda n, t: (0, 0)),
        pl.BlockSpec(w_rope.shape, lambda n, t: (0, 0)),
        pl.BlockSpec((cos_t.shape[0] // rb, LANES), lambda n, t: (t % rb, 0)),
        pl.BlockSpec((cos_t.shape[0] // rb, LANES), lambda n, t: (t % rb, 0)),
    ]
    args = [cq, w_nope, w_rope, cos_t, sin_t]
    if absorb:
        in_specs.append(pl.BlockSpec(w_ukt.shape, lambda n, t: (0, 0, 0)))
        args.append(w_ukt)
        widths = (KV_LORA, ROPE_D)
    else:
        widths = (HEAD_QK,)
    return pl.pallas_call(
        functools.partial(_q_kernel, absorb=absorb),
        grid=(nb, nt),
        in_specs=in_specs,
        out_specs=[pl.BlockSpec((None, MLA_HEADS, tm, w), lambda n, t: (n, 0, t, 0)) for w in widths],
        out_shape=[jax.ShapeDtypeStruct((nb, MLA_HEADS, seq, w), BF16) for w in widths],
        compiler_params=_cp(("parallel", "parallel")),
        name="mla_q_proj",
    )(*args)


def _lane_tile(x, width):
    return x if width == LANES else jnp.concatenate([x] * (width // LANES), axis=1)


def _prompt_attn_kernel(q_ref, k_ref, v_ref, o_ref, m_scr, l_scr, acc_scr, *, tq):
    i = pl.program_id(1)
    m_scr[...] = jnp.full_like(m_scr, NEG)
    l_scr[...] = jnp.zeros_like(l_scr)
    acc_scr[...] = jnp.zeros_like(acc_scr)

    def block(j, masked):
        r0 = pl.multiple_of(j * tq, tq)
        if masked:
            causal = (lax.broadcasted_iota(jnp.int32, (tq, tq), 1)
                      <= lax.broadcasted_iota(jnp.int32, (tq, tq), 0))

        def head(h, carry):
            s = _dot_nt(q_ref[h], k_ref[h, pl.ds(r0, tq), :]) * (MLA_SCALE * LOG2E)
            if masked:
                s = jnp.where(causal, s, NEG)
            m_prev = m_scr[h]
            m_new = jnp.maximum(m_prev, jnp.max(s, axis=-1, keepdims=True))
            alpha = jnp.exp2(m_prev - m_new)
            p = jnp.exp2(s - _lane_tile(m_new, tq))
            l_scr[h] = alpha * l_scr[h] + jnp.sum(p, axis=-1, keepdims=True)
            acc_scr[h] = alpha * acc_scr[h] + _dot(p.astype(BF16), v_ref[h, pl.ds(r0, tq), :])
            m_scr[h] = m_new
            return carry

        lax.fori_loop(0, MLA_HEADS, head, 0, unroll=True)

    def body(j, carry):
        block(j, False)
        return carry

    lax.fori_loop(0, i, body, 0)
    block(i, True)
    for h in range(MLA_HEADS):
        o_ref[:, h * V_D:(h + 1) * V_D] = (acc_scr[h] / l_scr[h]).astype(BF16)


def _prompt_attn(q_cat, k_cat, v_heads, nb, seq, tq):
    nq = seq // tq
    return pl.pallas_call(
        functools.partial(_prompt_attn_kernel, tq=tq),
        grid=(nb, nq),
        in_specs=[
            pl.BlockSpec((None, MLA_HEADS, tq, HEAD_QK), lambda n, i: (n, 0, i, 0)),
            pl.BlockSpec((None, MLA_HEADS, seq, HEAD_QK), lambda n, i: (n, 0, 0, 0)),
            pl.BlockSpec((None, MLA_HEADS, seq, V_D), lambda n, i: (n, 0, 0, 0)),
        ],
        out_specs=pl.BlockSpec((tq, MLA_HEADS * V_D), lambda n, i: (n * nq + i, 0)),
        out_shape=jax.ShapeDtypeStruct((nb * seq, MLA_HEADS * V_D), BF16),
        scratch_shapes=[pltpu.VMEM((MLA_HEADS, tq, LANES), F32),
                        pltpu.VMEM((MLA_HEADS, tq, LANES), F32),
                        pltpu.VMEM((MLA_HEADS, tq, V_D), F32)],
        compiler_params=_cp(("parallel", "arbitrary")),
        name="mla_prompt_attn",
    )(q_cat, k_cat, v_heads)


def _sample_attn_kernel(pt_ref, ql_ref, qr_ref, cn_ref, krn_ref, ckv_hbm, krt_hbm, o_ref,
                        kbuf, rbuf, kb16, kt16, sems, *, layer_j, n_pages):
    b = pl.program_id(0)
    nb = pl.num_programs(0)
    slot = b % 2

    def page_copies(seq, slot_, p):
        page = pt_ref[seq, p]
        return (
            pltpu.make_async_copy(ckv_hbm.at[layer_j, page],
                                  kbuf.at[slot_, p * PAGE_SIZE:(p + 1) * PAGE_SIZE, :],
                                  sems.at[0, slot_]),
            pltpu.make_async_copy(krt_hbm.at[layer_j, page],
                                  rbuf.at[slot_, :, p * PAGE_SIZE:(p + 1) * PAGE_SIZE],
                                  sems.at[1, slot_]),
        )

    def start_all(seq, slot_):
        for p in range(n_pages):
            for cp in page_copies(seq, slot_, p):
                cp.start()

    @pl.when(b == 0)
    def _():
        start_all(0, 0)

    @pl.when(b + 1 < nb)
    def _():
        start_all(b + 1, 1 - slot)

    for p in range(n_pages):
        for cp in page_copies(b, slot, p):
            cp.wait()

    ql = ql_ref[...]
    qr = qr_ref[...]
    past = n_pages * PAGE_SIZE
    tc = min(past, 1024)
    for c in range(past // tc):
        kb = kbuf[slot, c * tc:(c + 1) * tc, :].astype(BF16)
        kb16[c * tc:(c + 1) * tc, :] = kb
        kt16[:, c * tc:(c + 1) * tc] = kb.T
    s = (_dot(ql, kt16[...]) + _dot(qr, rbuf[slot].astype(BF16))) * MLA_SCALE
    cn = cn_ref[...]
    s_new = (jnp.sum(ql.astype(F32) * cn, axis=-1, keepdims=True)
             + jnp.sum(qr.astype(F32) * krn_ref[...], axis=-1, keepdims=True)) * MLA_SCALE
    m = jnp.maximum(jnp.max(s, axis=-1, keepdims=True), s_new)
    p = jnp.exp(s - m)
    p_new = jnp.exp(s_new - m)
    l = jnp.sum(p, axis=-1, keepdims=True) + p_new
    acc = _dot(p.astype(BF16), kb16[...]) + p_new * cn
    o_ref[...] = (acc / l).astype(BF16)


def _sample_attn(page_table, q_lat, q_rope, ckv_new, kr_new, cache_ckv, cache_krope_t, layer_j):
    nb, n_pages = page_table.shape
    past = n_pages * PAGE_SIZE
    grid_spec = pltpu.PrefetchScalarGridSpec(
        num_scalar_prefetch=1,
        grid=(nb,),
        in_specs=[
            pl.BlockSpec((None, MLA_HEADS, KV_LORA), lambda b, pt: (b, 0, 0)),
            pl.BlockSpec((None, MLA_HEADS, ROPE_D), lambda b, pt: (b, 0, 0)),
            pl.BlockSpec((None, 1, KV_LORA), lambda b, pt: (b, 0, 0)),
            pl.BlockSpec((None, 1, ROPE_D), lambda b, pt: (b, 0, 0)),
            pl.BlockSpec(memory_space=pl.ANY),
            pl.BlockSpec(memory_space=pl.ANY),
        ],
        out_specs=pl.BlockSpec((None, MLA_HEADS, KV_LORA), lambda b, pt: (b, 0, 0)),
        scratch_shapes=[
            pltpu.VMEM((2, past, KV_LORA), F32),
            pltpu.VMEM((2, ROPE_D, past), F32),
            pltpu.VMEM((past, KV_LORA), BF16),
            pltpu.VMEM((KV_LORA, past), BF16),
            pltpu.SemaphoreType.DMA((2, 2)),
        ],
    )
    return pl.pallas_call(
        functools.partial(_sample_attn_kernel, layer_j=layer_j, n_pages=n_pages),
        grid_spec=grid_spec,
        out_shape=jax.ShapeDtypeStruct((nb, MLA_HEADS, KV_LORA), BF16),
        compiler_params=_cp(("arbitrary",)),
        name="mla_sample_attn",
    )(page_table, q_lat, q_rope, ckv_new.reshape(nb, 1, KV_LORA), kr_new.reshape(nb, 1, ROPE_D),
      cache_ckv, cache_krope_t)


def _tile(m, pref):
    t = min(m, pref)
    assert m % t == 0, (m, t)
    return t


def kernel(x_prompt, x_sample, c_prompt, c_sample, cache_conv, state_hgrn, cache_ckv, cache_krope,
           page_table, w_ada, b_ada, norm_mix, norm_ffn, w_in_even, conv_w, hg_lb_logits, hg_gnorm,
           w_out_even, w_dqkv, q_norm, w_uq, kv_norm, w_uk, w_uv, w_o, w_gu, w_down, norm_final):
    bp, tp, d = x_prompt.shape
    bs, ts, _ = x_sample.shape
    assert ts == 1 and d == D_MODEL
    depth = w_ada.shape[0]
    n_pages = page_table.shape[1]
    past_len = n_pages * PAGE_SIZE
    mp, ms = bp * tp, bs

    tm_p = _tile(tp, 512)
    tm_s = ms
    tt = _tile(tp, 256)
    tq = _tile(tp, 512)
    tb = _tile(bs, 16)
    ns = 1
    krope_t = jnp.swapaxes(cache_krope, 2, 3)
    fc = 256

    mods = _ada_all(jnp.concatenate([c_prompt, c_sample], axis=0), w_ada, b_ada)
    mod_p = _Mod(mods[:, :bp].reshape(depth, bp, 1, N_MOD * d), tp // tm_p)
    mod_s = _Mod(mods[:, bp:].reshape(depth, 1, bs, N_MOD * d), 1)

    cos_p, sin_p = _rope_table(tp, 0, 1, _tile(tp, 512))
    cos_s, sin_s = _rope_table(8, past_len, 0, 8)
    cos_s = jnp.broadcast_to(cos_s[:1], (ms, LANES))
    sin_s = jnp.broadcast_to(sin_s[:1], (ms, LANES))

    w_in_b = w_in_even.astype(BF16)
    w_out_b = w_out_even.astype(BF16)
    w_gu_b = w_gu.astype(BF16)
    w_down_b = w_down.astype(BF16)
    w_o_b = w_o.astype(BF16)
    w_uvh_b = jnp.transpose(w_uv, (0, 2, 1, 3)).astype(BF16)

    xp = x_prompt.reshape(mp, d)
    xs = x_sample.reshape(ms, d)
    outs ={k: [] for k in ("conv_p", "hg_p", "ckv_p", "kr_p", "conv_s", "hg_s", "ckv_s", "kr_s")}

    for l in range(depth):
        j = l // 2
        g_mix = norm_mix[l].reshape(1, d)
        g_ffn = norm_ffn[l].reshape(1, d)
        if l % 2 == 0:
            gn = hg_gnorm[j].reshape(1, HG_DK)
            zp = _mod_matmul(xp, g_mix, mod_p, l, w_in_b, j, tm_p)
            zs = _mod_matmul(xs, g_mix, mod_s, l, w_in_b, j, tm_s)
            abp, cvp, hgp = _prompt_mixer(zp.reshape(bp, tp, EVEN_IN), conv_w[j], hg_lb_logits, gn,
                                          j, bp, tp, tt, ns)
            abp = abp.reshape(mp, d)
            abs_, cvs, hgs = _sample_mixer(zs, cache_conv, state_hgrn, conv_w[j], hg_lb_logits,
                                           gn, j, tb)
            mix_p, mix_s = (abp, w_out_b), (abs_, w_out_b)
            mix_w_specs_p = mix_w_specs = [_resident(w_out_b, j)]
            outs["conv_p"].append(cvp)
            outs["hg_p"].append(hgp)
            outs["conv_s"].append(cvs)
            outs["hg_s"].append(hgs)
        else:
            w_d = jnp.pad(w_dqkv[j], ((0, 0), (0, ROPE_D))).astype(BF16)
            wq = w_uq[j].reshape(Q_LORA, MLA_HEADS, NOPE + ROPE_D)
            w_nope = wq[:, :, :NOPE].reshape(Q_LORA, MLA_HEADS * NOPE).astype(BF16)
            w_rope = wq[:, :, NOPE:].reshape(Q_LORA, MLA_HEADS * ROPE_D).astype(BF16)
            w_ukt = jnp.transpose(w_uk[j], (1, 2, 0)).astype(BF16)
            qn = q_norm[j].reshape(1, Q_LORA)
            kvn = kv_norm[j].reshape(1, KV_LORA)

            w_uk_flat = w_uk[j].reshape(KV_LORA, MLA_HEADS * NOPE).astype(BF16)
            w_uv_flat = w_uv[j].reshape(KV_LORA, MLA_HEADS * V_D).astype(BF16)
            cqp, ckvp, krp, kcp, vhp = _dqkv(xp, g_mix, mod_p, l, w_d, qn, kvn, cos_p, sin_p, tm_p,
                                             tp // tm_p, tp, w_uk_flat, w_uv_flat)
            cqs, ckvs, krs = _dqkv(xs, g_mix, mod_s, l, w_d, qn, kvn, cos_s, sin_s, tm_s, 1)
            (qcp,) = _q_proj(cqp, w_nope, w_rope, None, cos_p, sin_p, bp, tp, tm_p, tp // tm_p)
            qls, qrs = _q_proj(cqs, w_nope, w_rope, w_ukt, cos_s, sin_s, 1, ms, tm_s, 1)
            ctxp = _prompt_attn(qcp, kcp, vhp, bp, tp, tq)
            ctxs = _sample_attn(page_table,
                                jnp.transpose(qls[0], (1, 0, 2)), jnp.transpose(qrs[0], (1, 0, 2)),
                                ckvs, krs, cache_ckv, krope_t, j)
            mix_p = (ctxp, w_o_b)
            mix_s = (ctxs.reshape(ms, MLA_HEADS * KV_LORA), w_uvh_b, w_o_b)
            mix_w_specs_p = [_resident(w_o_b, j)]
            mix_w_specs = [_resident(w_uvh_b, j), _resident(w_o_b, j)]
            outs["ckv_p"].append(ckvp.reshape(bp, tp, KV_LORA))
            outs["kr_p"].append(krp.reshape(bp, tp, ROPE_D))
            outs["ckv_s"].append(ckvs.reshape(bs, ts, KV_LORA))
            outs["kr_s"].append(krs.reshape(bs, ts, ROPE_D))

        last = l == depth - 1
        gfin = norm_final.reshape(1, d)
        def row_spec(a, tm):
            return pl.BlockSpec((tm, a.shape[1]), lambda i: (i, 0))

        xp = _ffn(mix_p, [row_spec(mix_p[0], tm_p)] + mix_w_specs_p, xp, g_ffn, mod_p, l,
                  w_gu_b, w_down_b, gfin, last, tm_p, fc)
        xs = _ffn(mix_s, [row_spec(mix_s[0], tm_s)] + mix_w_specs, xs, g_ffn, mod_s, l,
                  w_gu_b, w_down_b, gfin, last, tm_s, fc)

    return (xp.reshape(bp, tp, d), xs.reshape(bs, ts, d),
            jnp.stack(outs["conv_p"]), jnp.stack(outs["hg_p"]),
            jnp.stack(outs["ckv_p"]), jnp.stack(outs["kr_p"]),
            jnp.stack(outs["conv_s"]), jnp.stack(outs["hg_s"]),
            jnp.stack(outs["ckv_s"]), jnp.stack(outs["kr_s"]))
```

```python
import functools
import math

import jax
import jax.numpy as jnp
from jax import lax
from jax.experimental import pallas as pl
from jax.experimental.pallas import tpu as pltpu

F32 = jnp.float32
BF16 = jnp.bfloat16

D_MODEL = 1024
N_MOD = 6
EPS = 1e-6
NEG = -1e30
PAGE_SIZE = 128
CONV_CH = D_MODEL // 2
CONV_W = 3
HG_WIDTH = D_MODEL // 2
HG_DK = 128
HG_HEADS = HG_WIDTH // HG_DK
EVEN_IN = 3 * CONV_CH + 4 * HG_WIDTH
MLA_HEADS = 8
NOPE = 128
ROPE_D = 64
V_D = 128
Q_LORA = 384
KV_LORA = 256
ROPE_THETA = 10000.0
MLA_SCALE = (NOPE + ROPE_D) ** -0.5
HEAD_QK = 192
LANES = 128
SUBLANES = 8
LOG2E = 1.0 / math.log(2.0)

Z_B, Z_C, Z_X = 0, CONV_CH, 2 * CONV_CH
Z_Q = 3 * CONV_CH
Z_F = Z_Q + HG_WIDTH
Z_I = Z_F + HG_WIDTH
Z_G = Z_I + HG_WIDTH

HG_SUB = 16
HG_UNROLL = 2
VMEM_LIMIT = 56 * 1024 * 1024


def _cp(sem, vmem=VMEM_LIMIT):
    return pltpu.CompilerParams(dimension_semantics=sem, vmem_limit_bytes=vmem)


def _silu(x):
    return x * jax.nn.sigmoid(x)


def _rmsnorm(x, g):
    return x * lax.rsqrt(jnp.mean(x * x, axis=-1, keepdims=True) + EPS) * g


def _dot(a, b):
    return jnp.dot(a, b, preferred_element_type=F32)


def _dot_nt(a, b):
    return lax.dot_general(a, b, (((1,), (1,)), ((), ())), preferred_element_type=F32)


def _dot_tn(a, b):
    return lax.dot_general(a, b, (((0,), (0,)), ((), ())), preferred_element_type=F32)


def _ada_kernel(c_ref, w_ref, b_ref, o_ref):
    a = _silu(c_ref[...]).astype(BF16)
    o_ref[...] = _dot(a, w_ref[...].astype(BF16)) + b_ref[...]


def _ada_all(c_all, w_ada, b_ada, tn=1536):
    depth, d, n6 = w_ada.shape
    rows = c_all.shape[0]
    return pl.pallas_call(
        _ada_kernel,
        grid=(depth, n6 // tn),
        in_specs=[
            pl.BlockSpec((rows, d), lambda l, j: (0, 0)),
            pl.BlockSpec((None, d, tn), lambda l, j: (l, 0, j)),
            pl.BlockSpec((None, 1, tn), lambda l, j: (l, 0, j)),
        ],
        out_specs=pl.BlockSpec((None, rows, tn), lambda l, j: (l, 0, j)),
        out_shape=jax.ShapeDtypeStruct((depth, rows, n6), F32),
        compiler_params=_cp(("parallel", "parallel")),
        name="adaln_mod",
    )(c_all, w_ada, b_ada.reshape(depth, 1, n6))


class _Mod:
    def __init__(self, arr, tps):
        self.arr = arr
        self.tps = tps
        self.r = arr.shape[2]

    def spec(self, layer, col, width=D_MODEL, ncol=None):
        tps = self.tps
        per = D_MODEL // width
        if ncol is None:
            return pl.BlockSpec((None, None, self.r, width),
                                lambda i, *_: (layer, i // tps, 0, col * per))
        return pl.BlockSpec((None, None, self.r, width),
                            lambda i, j, *_: (layer, i // tps, 0, col * per + j))


def _resident(stacked, layer):
    shape = stacked.shape[1:]
    return pl.BlockSpec((None,) + shape, lambda *_: (layer,) + (0,) * len(shape),
                        pipeline_mode=pl.Buffered(1))


def _modmm_kernel(x_ref, g_ref, sh_ref, sc_ref, w_ref, o_ref):
    h = _rmsnorm(x_ref[...], g_ref[...]) * (1.0 + sc_ref[...]) + sh_ref[...]
    o_ref[...] = _dot(h.astype(BF16), w_ref[...])


def _mod_matmul(x, g, mod, layer, w_all, j, tm):
    m, d = x.shape
    n = w_all.shape[2]
    return pl.pallas_call(
        _modmm_kernel,
        grid=(m // tm,),
        in_specs=[
            pl.BlockSpec((tm, d), lambda i: (i, 0)),
            pl.BlockSpec((1, d), lambda i: (0, 0)),
            mod.spec(layer, 0),
            mod.spec(layer, 1),
            _resident(w_all, j),
        ],
        out_specs=pl.BlockSpec((tm, n), lambda i: (i, 0)),
        out_shape=jax.ShapeDtypeStruct((m, n), F32),
        compiler_params=_cp(("parallel",)),
        name="mod_matmul",
    )(x, g, mod.arr, mod.arr, w_all)


def _ffn_kernel(*refs, final_norm, dff, fc, n_mix):
    mix = refs[:n_mix]
    (x_ref, gmix_ref, g_ref, sh_ref, sc_ref, gate_ref, wgu_ref, wd_ref, gf_ref, o_ref,
     a_scr) = refs[n_mix:]
    if n_mix == 2:
        a_ref, wout_ref = mix
        y_mix = _dot(a_ref[...], wout_ref[...])
    else:
        ctx_ref, wuv_ref, wo_ref = mix
        parts = [_dot(ctx_ref[:, h * KV_LORA:(h + 1) * KV_LORA], wuv_ref[h]).astype(BF16)
                 for h in range(MLA_HEADS)]
        y_mix = _dot(jnp.concatenate(parts, axis=1), wo_ref[...])
    x = x_ref[...] + gmix_ref[...] * y_mix
    h = (_rmsnorm(x, g_ref[...]) * (1.0 + sc_ref[...]) + sh_ref[...]).astype(BF16)
    for c in range(dff // fc):
        gg = _dot(h, wgu_ref[:, c * fc:(c + 1) * fc])
        uu = _dot(h, wgu_ref[:, dff + c * fc:dff + (c + 1) * fc])
        a_scr[:, c * fc:(c + 1) * fc] = (_silu(gg) * uu).astype(BF16)
    y = x + gate_ref[...] * _dot(a_scr[...], wd_ref[...])
    if final_norm:
        y = _rmsnorm(y, gf_ref[...])
    o_ref[...] = y


def _ffn(mix, mix_specs, x, g, mod, layer, w_gu, w_down, g_final, final_norm, tm, fc):
    m, d = x.shape
    dff = w_down.shape[1]
    return pl.pallas_call(
        functools.partial(_ffn_kernel, final_norm=final_norm, dff=dff, fc=fc, n_mix=len(mix)),
        grid=(m // tm,),
        in_specs=list(mix_specs) + [
            pl.BlockSpec((tm, d), lambda i: (i, 0)),
            mod.spec(layer, 2),
            pl.BlockSpec((1, d), lambda i: (0, 0)),
            mod.spec(layer, 3),
            mod.spec(layer, 4),
            mod.spec(layer, 5),
            _resident(w_gu, layer),
            _resident(w_down, layer),
            pl.BlockSpec((1, d), lambda i: (0, 0)),
        ],
        out_specs=pl.BlockSpec((tm, d), lambda i: (i, 0)),
        out_shape=jax.ShapeDtypeStruct((m, d), F32),
        scratch_shapes=[pltpu.VMEM((tm, dff), BF16)],
        compiler_params=_cp(("parallel",)),
        name="mix_out_ffn",
    )(*mix, x, mod.arr, g, mod.arr, mod.arr, mod.arr, w_gu, w_down, g_final)


def _hg_lower_bound(lbl_ref, j):
    logits = lbl_ref[...]
    e = jnp.exp(logits - jnp.max(logits, axis=0, keepdims=True))
    den = jnp.sum(e, axis=0, keepdims=True)
    lb = jnp.zeros_like(den)
    for i in range(j):
        lb = lb + e[i:i + 1, :] / den
    return lb


def _cumsum_rows(x):
    rows = x.shape[0]
    idx = lax.broadcasted_iota(jnp.int32, x.shape, 0)
    d = 1
    while d < rows:
        x = x + jnp.where(idx >= d, pltpu.roll(x, d, 0), 0.0)
        d *= 2
    return x


def _hgrn_gates(qp, fp, lb):
    logf = jnp.log(lb + (1.0 - lb) * jax.nn.sigmoid(fp))
    kk = (1.0 - lb) * jax.nn.sigmoid(-fp)
    q = _silu(qp) * (HG_DK ** -0.5)
    return q, kk, logf


def _prompt_mixer_kernel(z_ref, cw_ref, lbl_ref, gn_ref, ab_ref, conv_ref, s_ref, ubuf, st_scr,
                         *, layer_j, tt, ns):
    t = pl.program_id(1)
    nt = pl.num_programs(1)

    @pl.when(t == 0)
    def _():
        ubuf[:, 0:8, :] = jnp.zeros((ns, 8, CONV_CH), F32)
        st_scr[...] = jnp.zeros_like(st_scr)

    for n in range(ns):
        u = z_ref[n, :, Z_C:Z_C + CONV_CH] * z_ref[n, :, Z_X:Z_X + CONV_CH]
        ubuf[n, 8:8 + tt, :] = u
        y = (cw_ref[0:1, :] * ubuf[n, 6:6 + tt, :] + cw_ref[1:2, :] * ubuf[n, 7:7 + tt, :]
             + cw_ref[2:3, :] * ubuf[n, 8:8 + tt, :])
        ab_ref[n, :, 0:CONV_CH] = (z_ref[n, :, Z_B:Z_B + CONV_CH] * y).astype(BF16)
        last2 = ubuf[n, tt + 6:tt + 8, :]
        ubuf[n, 6:8, :] = last2
        conv_ref[n] = last2

    lb_all = _hg_lower_bound(lbl_ref, layer_j)
    gn = gn_ref[...]
    ell = HG_SUB
    row8 = lax.broadcasted_iota(jnp.int32, (SUBLANES, HG_DK), 0)

    def chunk(c, carry):
        r0 = pl.multiple_of(c * ell, ell)
        for n in range(ns):
            for h in range(HG_HEADS):
                lo = h * HG_DK
                lb = lb_all[:, lo:lo + HG_DK]
                qp = z_ref[n, pl.ds(r0, ell), Z_Q + lo:Z_Q + lo + HG_DK]
                fp = z_ref[n, pl.ds(r0, ell), Z_F + lo:Z_F + lo + HG_DK]
                v = z_ref[n, pl.ds(r0, ell), Z_I + lo:Z_I + lo + HG_DK]
                gp = z_ref[n, pl.ds(r0, ell), Z_G + lo:Z_G + lo + HG_DK]
                q, kk, logf = _hgrn_gates(qp, fp, lb)
                gc = _cumsum_rows(logf)
                gl = gc[ell - 1:ell, :]
                st = st_scr[n, h]
                o = _dot_nt((q * jnp.exp(gc)).astype(BF16), st.astype(BF16))
                o_grp = [o[r:r + SUBLANES, :] for r in range(0, ell, SUBLANES)]
                gc2 = gc * LOG2E
                for s in range(ell):
                    for gi, r in enumerate(range(0, ell, SUBLANES)):
                        if r + SUBLANES <= s:
                            continue
                        d = gc2[r:r + SUBLANES, :] - gc2[s:s + 1, :]
                        if r <= s:
                            d = jnp.where(row8 >= s - r, d, NEG)
                        w = q[r:r + SUBLANES, :] * kk[s:s + 1, :] * jnp.exp2(d)
                        o_grp[gi] = (o_grp[gi]
                                     + jnp.sum(w, axis=-1, keepdims=True) * v[s:s + 1, :])
                o = jnp.concatenate(o_grp, axis=0)
                kd = kk * jnp.exp(gl - gc)
                st_scr[n, h] = st * jnp.exp(gl) + _dot_tn(v.astype(BF16), kd.astype(BF16))
                b = _rmsnorm(o, gn) * _silu(gp)
                ab_ref[n, pl.ds(r0, ell), CONV_CH + lo:CONV_CH + lo + HG_DK] = b.astype(BF16)
        return carry

    lax.fori_loop(0, tt // ell, chunk, 0, unroll=HG_UNROLL)

    @pl.when(t == nt - 1)
    def _():
        for n in range(ns):
            for h in range(HG_HEADS):
                s_ref[n, h] = st_scr[n, h].T


def _prompt_mixer(z, conv_w_j, lb_logits, gnorm_j, layer_j, nb, seq, tt, ns):
    nt = seq // tt
    return pl.pallas_call(
        functools.partial(_prompt_mixer_kernel, layer_j=layer_j, tt=tt, ns=ns),
        grid=(nb // ns, nt),
        in_specs=[
            pl.BlockSpec((ns, tt, EVEN_IN), lambda g, t: (g, t, 0)),
            pl.BlockSpec((CONV_W, CONV_CH), lambda g, t: (0, 0)),
            pl.BlockSpec(lb_logits.shape, lambda g, t: (0, 0)),
            pl.BlockSpec((1, HG_DK), lambda g, t: (0, 0)),
        ],
        out_specs=[
            pl.BlockSpec((ns, tt, D_MODEL), lambda g, t: (g, t, 0)),
            pl.BlockSpec((ns, CONV_W - 1, CONV_CH), lambda g, t: (g, 0, 0)),
            pl.BlockSpec((ns, HG_HEADS, HG_DK, HG_DK), lambda g, t: (g, 0, 0, 0)),
        ],
        out_shape=[
            jax.ShapeDtypeStruct((nb, seq, D_MODEL), BF16),
            jax.ShapeDtypeStruct((nb, CONV_W - 1, CONV_CH), F32),
            jax.ShapeDtypeStruct((nb, HG_HEADS, HG_DK, HG_DK), F32),
        ],
        scratch_shapes=[pltpu.VMEM((ns, tt + 8, CONV_CH), F32),
                        pltpu.VMEM((ns, HG_HEADS, HG_DK, HG_DK), F32)],
        compiler_params=_cp(("parallel", "arbitrary")),
        name="prompt_conv_hgrn",
    )(z, conv_w_j, lb_logits, gnorm_j)


def _column(row_vec, eye):
    return jnp.sum(jnp.where(eye, row_vec, 0.0), axis=1, keepdims=True)


def _sample_mixer_kernel(z_ref, cb_ref, s0_ref, cw_ref, lbl_ref, gn_ref, ab_ref, conv_ref, s_ref,
                         b_scr, *, layer_j, tb):
    u = z_ref[:, Z_C:Z_C + CONV_CH] * z_ref[:, Z_X:Z_X + CONV_CH]
    b0 = cb_ref[:, 0, :]
    b1 = cb_ref[:, 1, :]
    y = cw_ref[0:1, :] * b0 + cw_ref[1:2, :] * b1 + cw_ref[2:3, :] * u
    ab_ref[:, 0:CONV_CH] = (z_ref[:, Z_B:Z_B + CONV_CH] * y).astype(BF16)
    conv_ref[:, 0, :] = b1
    conv_ref[:, 1, :] = u

    lb_all = _hg_lower_bound(lbl_ref, layer_j)
    gn = gn_ref[...]
    eye = (lax.broadcasted_iota(jnp.int32, (HG_DK, HG_DK), 0)
           == lax.broadcasted_iota(jnp.int32, (HG_DK, HG_DK), 1))

    q_all, kk_all, logf_all = _hgrn_gates(z_ref[:, Z_Q:Z_Q + HG_WIDTH], z_ref[:, Z_F:Z_F + HG_WIDTH],
                                          lb_all)
    ef_all = jnp.exp(logf_all)
    v_all = z_ref[:, Z_I:Z_I + HG_WIDTH]
    for n in range(tb):
        for h in range(HG_HEADS):
            lo = h * HG_DK
            row = lambda a: a[n:n + 1, lo:lo + HG_DK]
            s_new = (_column(row(ef_all), eye) * s0_ref[n, h]
                     + _column(row(kk_all), eye) * row(v_all))
            s_ref[n, h] = s_new
            b_scr[n:n + 1, lo:lo + HG_DK] = jnp.sum(_column(row(q_all), eye) * s_new, axis=0,
                                                    keepdims=True)
    for h in range(HG_HEADS):
        lo = h * HG_DK
        b = _rmsnorm(b_scr[:, lo:lo + HG_DK], gn) * _silu(z_ref[:, Z_G + lo:Z_G + lo + HG_DK])
        ab_ref[:, CONV_CH + lo:CONV_CH + lo + HG_DK] = b.astype(BF16)


def _sample_mixer(z, conv_buf, s0, conv_w_j, lb_logits, gnorm_j, layer_j, tb):
    nb = z.shape[0]
    return pl.pallas_call(
        functools.partial(_sample_mixer_kernel, layer_j=layer_j, tb=tb),
        grid=(nb // tb,),
        in_specs=[
            pl.BlockSpec((tb, EVEN_IN), lambda i: (i, 0)),
            pl.BlockSpec((None, tb, CONV_W - 1, CONV_CH), lambda i: (layer_j, i, 0, 0)),
            pl.BlockSpec((None, tb, HG_HEADS, HG_DK, HG_DK), lambda i: (layer_j, i, 0, 0, 0)),
            pl.BlockSpec((CONV_W, CONV_CH), lambda i: (0, 0)),
            pl.BlockSpec(lb_logits.shape, lambda i: (0, 0)),
            pl.BlockSpec((1, HG_DK), lambda i: (0, 0)),
        ],
        out_specs=[
            pl.BlockSpec((tb, D_MODEL), lambda i: (i, 0)),
            pl.BlockSpec((tb, CONV_W - 1, CONV_CH), lambda i: (i, 0, 0)),
            pl.BlockSpec((tb, HG_HEADS, HG_DK, HG_DK), lambda i: (i, 0, 0, 0)),
        ],
        out_shape=[
            jax.ShapeDtypeStruct((nb, D_MODEL), BF16),
            jax.ShapeDtypeStruct((nb, CONV_W - 1, CONV_CH), F32),
            jax.ShapeDtypeStruct((nb, HG_HEADS, HG_DK, HG_DK), F32),
        ],
        scratch_shapes=[pltpu.VMEM((tb, HG_WIDTH), F32)],
        compiler_params=_cp(("parallel",)),
        name="sample_conv_hgrn",
    )(z, conv_buf, s0, conv_w_j, lb_logits, gnorm_j)


def _rope_table_kernel(cos_ref, sin_ref, *, tr, pos0, step):
    lane = lax.broadcasted_iota(jnp.int32, (tr, LANES), 1)
    rowi = lax.broadcasted_iota(jnp.int32, (tr, LANES), 0)
    half = ROPE_D // 2
    fi = (lane % half).astype(F32)
    inv = jnp.exp(fi * (-math.log(ROPE_THETA) / half))
    pos = (pos0 + step * (pl.program_id(0) * tr + rowi)).astype(F32)
    ang = pos * inv
    sign = jnp.where((lane % ROPE_D) < half, -1.0, 1.0)
    cos_ref[...] = jnp.cos(ang)
    sin_ref[...] = jnp.sin(ang) * sign


def _rope_table(rows, pos0, step, tr):
    return pl.pallas_call(
        functools.partial(_rope_table_kernel, tr=tr, pos0=pos0, step=step),
        grid=(rows // tr,),
        out_specs=[pl.BlockSpec((tr, LANES), lambda i: (i, 0))] * 2,
        out_shape=[jax.ShapeDtypeStruct((rows, LANES), F32)] * 2,
        compiler_params=_cp(("parallel",)),
        name="rope_table",
    )()


def _rope_pairs(g, cos, sin_signed):
    lane = lax.broadcasted_iota(jnp.int32, g.shape, 1)
    half = ROPE_D // 2
    n = g.shape[1]
    rot = jnp.where((lane % ROPE_D) < half, pltpu.roll(g, n - half, 1), pltpu.roll(g, half, 1))
    return g * cos + rot * sin_signed


def _dqkv_kernel(x_ref, g_ref, sh_ref, sc_ref, w_ref, qn_ref, kvn_ref, cos_ref, sin_ref, *rest,
                 per_head_kv):
    if per_head_kv:
        wuk_ref, wuv_ref, cq_ref, ckv_ref, kr_ref, kc_ref, v_ref = rest
    else:
        cq_ref, ckv_ref, kr_ref = rest
    tm = x_ref.shape[0]
    sub = min(tm, 256)
    for r in range(0, tm, sub):
        rows = slice(r, r + sub)
        sc = sc_ref[...] if sc_ref.shape[0] == 1 else sc_ref[rows, :]
        sh = sh_ref[...] if sh_ref.shape[0] == 1 else sh_ref[rows, :]
        h = _rmsnorm(x_ref[rows, :], g_ref[...]) * (1.0 + sc) + sh
        d = _dot(h.astype(BF16), w_ref[...])
        cq_ref[rows, :] = _rmsnorm(d[:, :Q_LORA], qn_ref[...]).astype(BF16)
        ckv = _rmsnorm(d[:, Q_LORA:Q_LORA + KV_LORA], kvn_ref[...])
        ckv_ref[rows, :] = ckv
        kr = _rope_pairs(d[:, Q_LORA + KV_LORA:], cos_ref[rows, :], sin_ref[rows, :])[:, :ROPE_D]
        kr_ref[rows, :] = kr
        if per_head_kv:
            ckv_b = ckv.astype(BF16)
            k_all = _dot(ckv_b, wuk_ref[...]).astype(BF16)
            v_all = _dot(ckv_b, wuv_ref[...]).astype(BF16)
            tail = kr.astype(BF16)
            for hd in range(MLA_HEADS):
                kc_ref[hd, rows, 0:NOPE] = k_all[:, hd * NOPE:(hd + 1) * NOPE]
                kc_ref[hd, rows, NOPE:] = tail
                v_ref[hd, rows, :] = v_all[:, hd * V_D:(hd + 1) * V_D]


def _dqkv(x, g, mod, layer, w_pad, q_norm, kv_norm, cos_t, sin_t, tm, rope_blocks, seq=None,
          w_uk_flat=None, w_uv_flat=None):
    m, d = x.shape
    n = w_pad.shape[1]
    rb = rope_blocks
    per_head_kv = w_uk_flat is not None
    in_specs = [
        pl.BlockSpec((tm, d), lambda i: (i, 0)),
        pl.BlockSpec((1, d), lambda i: (0, 0)),
        mod.spec(layer, 0),
        mod.spec(layer, 1),
        pl.BlockSpec((d, n), lambda i: (0, 0)),
        pl.BlockSpec((1, Q_LORA), lambda i: (0, 0)),
        pl.BlockSpec((1, KV_LORA), lambda i: (0, 0)),
        pl.BlockSpec((cos_t.shape[0] // rb, LANES), lambda i: (i % rb, 0)),
        pl.BlockSpec((cos_t.shape[0] // rb, LANES), lambda i: (i % rb, 0)),
    ]
    out_specs = [
        pl.BlockSpec((tm, Q_LORA), lambda i: (i, 0)),
        pl.BlockSpec((tm, KV_LORA), lambda i: (i, 0)),
        pl.BlockSpec((tm, ROPE_D), lambda i: (i, 0)),
    ]
    out_shape = [
        jax.ShapeDtypeStruct((m, Q_LORA), BF16),
        jax.ShapeDtypeStruct((m, KV_LORA), F32),
        jax.ShapeDtypeStruct((m, ROPE_D), F32),
    ]
    args = [x, g, mod.arr, mod.arr, w_pad, q_norm, kv_norm, cos_t, sin_t]
    if per_head_kv:
        tps = seq // tm
        nb = m // seq
        in_specs += [pl.BlockSpec(w_uk_flat.shape, lambda i: (0, 0)),
                     pl.BlockSpec(w_uv_flat.shape, lambda i: (0, 0))]
        out_specs += [
            pl.BlockSpec((None, MLA_HEADS, tm, HEAD_QK), lambda i: (i // tps, 0, i % tps, 0)),
            pl.BlockSpec((None, MLA_HEADS, tm, V_D), lambda i: (i // tps, 0, i % tps, 0)),
        ]
        out_shape += [
            jax.ShapeDtypeStruct((nb, MLA_HEADS, seq, HEAD_QK), BF16),
            jax.ShapeDtypeStruct((nb, MLA_HEADS, seq, V_D), BF16),
        ]
        args += [w_uk_flat, w_uv_flat]
    return pl.pallas_call(
        functools.partial(_dqkv_kernel, per_head_kv=per_head_kv),
        grid=(m // tm,),
        in_specs=in_specs,
        out_specs=out_specs,
        out_shape=out_shape,
        compiler_params=_cp(("parallel",)),
        name="mla_down_proj",
    )(*args)


def _q_kernel(cq_ref, wn_ref, wr_ref, cos_ref, sin_ref, *rest, absorb):
    cq = cq_ref[...]
    qn = _dot(cq, wn_ref[...])
    qr = _dot(cq, wr_ref[...])
    cos = jnp.concatenate([cos_ref[...]] * (MLA_HEADS * ROPE_D // LANES), axis=1)
    sin = jnp.concatenate([sin_ref[...]] * (MLA_HEADS * ROPE_D // LANES), axis=1)
    qr = _rope_pairs(qr, cos, sin).astype(BF16)
    if absorb:
        wuk_ref, ql_ref, qr_ref = rest
        for h in range(MLA_HEADS):
            ql_ref[h] = _dot(qn[:, h * NOPE:(h + 1) * NOPE].astype(BF16), wuk_ref[h]).astype(BF16)
            qr_ref[h] = qr[:, h * ROPE_D:(h + 1) * ROPE_D]
    else:
        (qc_ref,) = rest
        qn = qn.astype(BF16)
        for h in range(MLA_HEADS):
            qc_ref[h, :, 0:NOPE] = qn[:, h * NOPE:(h + 1) * NOPE]
            qc_ref[h, :, NOPE:] = qr[:, h * ROPE_D:(h + 1) * ROPE_D]


def _q_proj(cq, w_nope, w_rope, w_ukt, cos_t, sin_t, nb, seq, tm, rope_blocks):
    nt = seq // tm
    rb = rope_blocks
    absorb = w_ukt is not None
    in_specs = [
        pl.BlockSpec((tm, Q_LORA), lambda n, t: (n * nt + t, 0)),
        pl.BlockSpec(w_nope.shape, lambda n, t: (0, 0)),
        pl.BlockSpec(w_rope.shape, lambda n, t: (0, 0)),
        pl.BlockSpec((cos_t.shape[0] // rb, LANES), lambda n, t: (t % rb, 0)),
        pl.BlockSpec((cos_t.shape[0] // rb, LANES), lambda n, t: (t % rb, 0)),
    ]
    args = [cq, w_nope, w_rope, cos_t, sin_t]
    if absorb:
        in_specs.append(pl.BlockSpec(w_ukt.shape, lambda n, t: (0, 0, 0)))
        args.append(w_ukt)
        widths = (KV_LORA, ROPE_D)
    else:
        widths = (HEAD_QK,)
    return pl.pallas_call(
        functools.partial(_q_kernel, absorb=absorb),
        grid=(nb, nt),
        in_specs=in_specs,
        out_specs=[pl.BlockSpec((None, MLA_HEADS, tm, w), lambda n, t: (n, 0, t, 0)) for w in widths],
        out_shape=[jax.ShapeDtypeStruct((nb, MLA_HEADS, seq, w), BF16) for w in widths],
        compiler_params=_cp(("parallel", "parallel")),
        name="mla_q_proj",
    )(*args)


def _lane_tile(x, width):
    return x if width == LANES else jnp.concatenate([x] * (width // LANES), axis=1)


def _prompt_attn_kernel(q_ref, k_ref, v_ref, o_ref, m_scr, l_scr, acc_scr, *, tq):
    i = pl.program_id(1)
    m_scr[...] = jnp.full_like(m_scr, NEG)
    l_scr[...] = jnp.zeros_like(l_scr)
    acc_scr[...] = jnp.zeros_like(acc_scr)

    def block(j, masked):
        r0 = pl.multiple_of(j * tq, tq)
        if masked:
            causal = (lax.broadcasted_iota(jnp.int32, (tq, tq), 1)
                      <= lax.broadcasted_iota(jnp.int32, (tq, tq), 0))

        def head(h, carry):
            s = _dot_nt(q_ref[h], k_ref[h, pl.ds(r0, tq), :]) * (MLA_SCALE * LOG2E)
            if masked:
                s = jnp.where(causal, s, NEG)
            m_prev = m_scr[h]
            m_new = jnp.maximum(m_prev, jnp.max(s, axis=-1, keepdims=True))
            alpha = jnp.exp2(m_prev - m_new)
            p = jnp.exp2(s - _lane_tile(m_new, tq))
            l_scr[h] = alpha * l_scr[h] + jnp.sum(p, axis=-1, keepdims=True)
            acc_scr[h] = alpha * acc_scr[h] + _dot(p.astype(BF16), v_ref[h, pl.ds(r0, tq), :])
            m_scr[h] = m_new
            return carry

        lax.fori_loop(0, MLA_HEADS, head, 0, unroll=True)

    def body(j, carry):
        block(j, False)
        return carry

    lax.fori_loop(0, i, body, 0)
    block(i, True)
    for h in range(MLA_HEADS):
        o_ref[:, h * V_D:(h + 1) * V_D] = (acc_scr[h] / l_scr[h]).astype(BF16)


def _prompt_attn(q_cat, k_cat, v_heads, nb, seq, tq):
    nq = seq // tq
    return pl.pallas_call(
        functools.partial(_prompt_attn_kernel, tq=tq),
        grid=(nb, nq),
        in_specs=[
            pl.BlockSpec((None, MLA_HEADS, tq, HEAD_QK), lambda n, i: (n, 0, i, 0)),
            pl.BlockSpec((None, MLA_HEADS, seq, HEAD_QK), lambda n, i: (n, 0, 0, 0)),
            pl.BlockSpec((None, MLA_HEADS, seq, V_D), lambda n, i: (n, 0, 0, 0)),
        ],
        out_specs=pl.BlockSpec((tq, MLA_HEADS * V_D), lambda n, i: (n * nq + i, 0)),
        out_shape=jax.ShapeDtypeStruct((nb * seq, MLA_HEADS * V_D), BF16),
        scratch_shapes=[pltpu.VMEM((MLA_HEADS, tq, LANES), F32),
                        pltpu.VMEM((MLA_HEADS, tq, LANES), F32),
                        pltpu.VMEM((MLA_HEADS, tq, V_D), F32)],
        compiler_params=_cp(("parallel", "arbitrary")),
        name="mla_prompt_attn",
    )(q_cat, k_cat, v_heads)


def _sample_attn_kernel(pt_ref, ql_ref, qr_ref, cn_ref, krn_ref, ckv_hbm, krt_hbm, o_ref,
                        kbuf, rbuf, kb16, kt16, sems, *, layer_j, n_pages):
    b = pl.program_id(0)
    nb = pl.num_programs(0)
    slot = b % 2

    def page_copies(seq, slot_, p):
        page = pt_ref[seq, p]
        return (
            pltpu.make_async_copy(ckv_hbm.at[layer_j, page],
                                  kbuf.at[slot_, p * PAGE_SIZE:(p + 1) * PAGE_SIZE, :],
                                  sems.at[0, slot_]),
            pltpu.make_async_copy(krt_hbm.at[layer_j, page],
                                  rbuf.at[slot_, :, p * PAGE_SIZE:(p + 1) * PAGE_SIZE],
                                  sems.at[1, slot_]),
        )

    def start_all(seq, slot_):
        for p in range(n_pages):
            for cp in page_copies(seq, slot_, p):
                cp.start(priority=p % 2)

    @pl.when(b == 0)
    def _():
        start_all(0, 0)

    @pl.when(b + 1 < nb)
    def _():
        start_all(b + 1, 1 - slot)

    for p in range(n_pages):
        for cp in page_copies(b, slot, p):
            cp.wait()

    ql = ql_ref[...]
    qr = qr_ref[...]
    past = n_pages * PAGE_SIZE
    tc = min(past, 1024)
    for c in range(past // tc):
        kb = kbuf[slot, c * tc:(c + 1) * tc, :].astype(BF16)
        kb16[c * tc:(c + 1) * tc, :] = kb
        kt16[:, c * tc:(c + 1) * tc] = kb.T
    s = (_dot(ql, kt16[...]) + _dot(qr, rbuf[slot].astype(BF16))) * MLA_SCALE
    cn = cn_ref[...]
    s_new = (jnp.sum(ql.astype(F32) * cn, axis=-1, keepdims=True)
             + jnp.sum(qr.astype(F32) * krn_ref[...], axis=-1, keepdims=True)) * MLA_SCALE
    m = jnp.maximum(jnp.max(s, axis=-1, keepdims=True), s_new)
    p = jnp.exp(s - m)
    p_new = jnp.exp(s_new - m)
    l = jnp.sum(p, axis=-1, keepdims=True) + p_new
    acc = _dot(p.astype(BF16), kb16[...]) + p_new * cn
    o_ref[...] = (acc / l).astype(BF16)


def _sample_attn(page_table, q_lat, q_rope, ckv_new, kr_new, cache_ckv, cache_krope_t, layer_j):
    nb, n_pages = page_table.shape
    past = n_pages * PAGE_SIZE
    grid_spec = pltpu.PrefetchScalarGridSpec(
        num_scalar_prefetch=1,
        grid=(nb,),
        in_specs=[
            pl.BlockSpec((None, MLA_HEADS, KV_LORA), lambda b, pt: (b, 0, 0)),
            pl.BlockSpec((None, MLA_HEADS, ROPE_D), lambda b, pt: (b, 0, 0)),
            pl.BlockSpec((None, 1, KV_LORA), lambda b, pt: (b, 0, 0)),
            pl.BlockSpec((None, 1, ROPE_D), lambda b, pt: (b, 0, 0)),
            pl.BlockSpec(memory_space=pl.ANY),
            pl.BlockSpec(memory_space=pl.ANY),
        ],
        out_specs=pl.BlockSpec((None, MLA_HEADS, KV_LORA), lambda b, pt: (b, 0, 0)),
        scratch_shapes=[
            pltpu.VMEM((2, past, KV_LORA), F32),
            pltpu.VMEM((2, ROPE_D, past), F32),
            pltpu.VMEM((past, KV_LORA), BF16),
            pltpu.VMEM((KV_LORA, past), BF16),
            pltpu.SemaphoreType.DMA((2, 2)),
        ],
    )
    return pl.pallas_call(
        functools.partial(_sample_attn_kernel, layer_j=layer_j, n_pages=n_pages),
        grid_spec=grid_spec,
        out_shape=jax.ShapeDtypeStruct((nb, MLA_HEADS, KV_LORA), BF16),
        compiler_params=_cp(("arbitrary",)),
        name="mla_sample_attn",
    )(page_table, q_lat, q_rope, ckv_new.reshape(nb, 1, KV_LORA), kr_new.reshape(nb, 1, ROPE_D),
      cache_ckv, cache_krope_t)


def _tile(m, pref):
    t = min(m, pref)
    assert m % t == 0, (m, t)
    return t


def kernel(x_prompt, x_sample, c_prompt, c_sample, cache_conv, state_hgrn, cache_ckv, cache_krope,
           page_table, w_ada, b_ada, norm_mix, norm_ffn, w_in_even, conv_w, hg_lb_logits, hg_gnorm,
           w_out_even, w_dqkv, q_norm, w_uq, kv_norm, w_uk, w_uv, w_o, w_gu, w_down, norm_final):
    bp, tp, d = x_prompt.shape
    bs, ts, _ = x_sample.shape
    assert ts == 1 and d == D_MODEL
    depth = w_ada.shape[0]
    n_pages = page_table.shape[1]
    past_len = n_pages * PAGE_SIZE
    mp, ms = bp * tp, bs

    tm_p = _tile(tp, 512)
    tm_s = ms
    tt = _tile(tp, 256)
    tq = _tile(tp, 512)
    tb = _tile(bs, 16)
    ns = 1
    krope_t = jnp.swapaxes(cache_krope, 2, 3)
    fc = 256

    mods = _ada_all(jnp.concatenate([c_prompt, c_sample], axis=0), w_ada, b_ada)
    mod_p = _Mod(mods[:, :bp].reshape(depth, bp, 1, N_MOD * d), tp // tm_p)
    mod_s = _Mod(mods[:, bp:].reshape(depth, 1, bs, N_MOD * d), 1)

    cos_p, sin_p = _rope_table(tp, 0, 1, _tile(tp, 512))
    cos_s, sin_s = _rope_table(8, past_len, 0, 8)
    cos_s = jnp.broadcast_to(cos_s[:1], (ms, LANES))
    sin_s = jnp.broadcast_to(sin_s[:1], (ms, LANES))

    w_in_b = w_in_even.astype(BF16)
    w_out_b = w_out_even.astype(BF16)
    w_gu_b = w_gu.astype(BF16)
    w_down_b = w_down.astype(BF16)
    w_o_b = w_o.astype(BF16)
    w_uvh_b = jnp.transpose(w_uv, (0, 2, 1, 3)).astype(BF16)

    xp = x_prompt.reshape(mp, d)
    xs = x_sample.reshape(ms, d)
    outs ={k: [] for k in ("conv_p", "hg_p", "ckv_p", "kr_p", "conv_s", "hg_s", "ckv_s", "kr_s")}

    for l in range(depth):
        j = l // 2
        g_mix = norm_mix[l].reshape(1, d)
        g_ffn = norm_ffn[l].reshape(1, d)
        if l % 2 == 0:
            gn = hg_gnorm[j].reshape(1, HG_DK)
            zp = _mod_matmul(xp, g_mix, mod_p, l, w_in_b, j, tm_p)
            zs = _mod_matmul(xs, g_mix, mod_s, l, w_in_b, j, tm_s)
            abp, cvp, hgp = _prompt_mixer(zp.reshape(bp, tp, EVEN_IN), conv_w[j], hg_lb_logits, gn,
                                          j, bp, tp, tt, ns)
            abp = abp.reshape(mp, d)
            abs_, cvs, hgs = _sample_mixer(zs, cache_conv, state_hgrn, conv_w[j], hg_lb_logits,
                                           gn, j, tb)
            mix_p, mix_s = (abp, w_out_b), (abs_, w_out_b)
            mix_w_specs_p = mix_w_specs = [_resident(w_out_b, j)]
            outs["conv_p"].append(cvp)
            outs["hg_p"].append(hgp)
            outs["conv_s"].append(cvs)
            outs["hg_s"].append(hgs)
        else:
            w_d = jnp.pad(w_dqkv[j], ((0, 0), (0, ROPE_D))).astype(BF16)
            wq = w_uq[j].reshape(Q_LORA, MLA_HEADS, NOPE + ROPE_D)
            w_nope = wq[:, :, :NOPE].reshape(Q_LORA, MLA_HEADS * NOPE).astype(BF16)
            w_rope = wq[:, :, NOPE:].reshape(Q_LORA, MLA_HEADS * ROPE_D).astype(BF16)
            w_ukt = jnp.transpose(w_uk[j], (1, 2, 0)).astype(BF16)
            qn = q_norm[j].reshape(1, Q_LORA)
            kvn = kv_norm[j].reshape(1, KV_LORA)

            w_uk_flat = w_uk[j].reshape(KV_LORA, MLA_HEADS * NOPE).astype(BF16)
            w_uv_flat = w_uv[j].reshape(KV_LORA, MLA_HEADS * V_D).astype(BF16)
            cqp, ckvp, krp, kcp, vhp = _dqkv(xp, g_mix, mod_p, l, w_d, qn, kvn, cos_p, sin_p, tm_p,
                                             tp // tm_p, tp, w_uk_flat, w_uv_flat)
            cqs, ckvs, krs = _dqkv(xs, g_mix, mod_s, l, w_d, qn, kvn, cos_s, sin_s, tm_s, 1)
            (qcp,) = _q_proj(cqp, w_nope, w_rope, None, cos_p, sin_p, bp, tp, tm_p, tp // tm_p)
            qls, qrs = _q_proj(cqs, w_nope, w_rope, w_ukt, cos_s, sin_s, 1, ms, tm_s, 1)
            ctxp = _prompt_attn(qcp, kcp, vhp, bp, tp, tq)
            ctxs = _sample_attn(page_table,
                                jnp.transpose(qls[0], (1, 0, 2)), jnp.transpose(qrs[0], (1, 0, 2)),
                                ckvs, krs, cache_ckv, krope_t, j)
            mix_p = (ctxp, w_o_b)
            mix_s = (ctxs.reshape(ms, MLA_HEADS * KV_LORA), w_uvh_b, w_o_b)
            mix_w_specs_p = [_resident(w_o_b, j)]
            mix_w_specs = [_resident(w_uvh_b, j), _resident(w_o_b, j)]
            outs["ckv_p"].append(ckvp.reshape(bp, tp, KV_LORA))
            outs["kr_p"].append(krp.reshape(bp, tp, ROPE_D))
            outs["ckv_s"].append(ckvs.reshape(bs, ts, KV_LORA))
            outs["kr_s"].append(krs.reshape(bs, ts, ROPE_D))

        last = l == depth - 1
        gfin = norm_final.reshape(1, d)
        def row_spec(a, tm):
            return pl.BlockSpec((tm, a.shape[1]), lambda i: (i, 0))

        xp = _ffn(mix_p, [row_spec(mix_p[0], tm_p)] + mix_w_specs_p, xp, g_ffn, mod_p, l,
                  w_gu_b, w_down_b, gfin, last, tm_p, fc)
        xs = _ffn(mix_s, [row_spec(mix_s[0], tm_s)] + mix_w_specs, xs, g_ffn, mod_s, l,
                  w_gu_b, w_down_b, gfin, last, tm_s, fc)

    return (xp.reshape(bp, tp, d), xs.reshape(bs, ts, d),
            jnp.stack(outs["conv_p"]), jnp.stack(outs["hg_p"]),
            jnp.stack(outs["ckv_p"]), jnp.stack(outs["kr_p"]),
            jnp.stack(outs["conv_s"]), jnp.stack(outs["hg_s"]),
            jnp.stack(outs["ckv_s"]), jnp.stack(outs["kr_s"]))
```

```python
import functools
import math

import jax
import jax.numpy as jnp
from jax import lax
from jax.experimental import pallas as pl
from jax.experimental.pallas import tpu as pltpu

F32 = jnp.float32
BF16 = jnp.bfloat16

D_MODEL = 1024
N_MOD = 6
EPS = 1e-6
NEG = -1e30
PAGE_SIZE = 128
CONV_CH = D_MODEL // 2
CONV_W = 3
HG_WIDTH = D_MODEL // 2
HG_DK = 128
HG_HEADS = HG_WIDTH // HG_DK
EVEN_IN = 3 * CONV_CH + 4 * HG_WIDTH
MLA_HEADS = 8
NOPE = 128
ROPE_D = 64
V_D = 128
Q_LORA = 384
KV_LORA = 256
ROPE_THETA = 10000.0
MLA_SCALE = (NOPE + ROPE_D) ** -0.5
HEAD_QK = 192
LANES = 128
SUBLANES = 8
LOG2E = 1.0 / math.log(2.0)

Z_B, Z_C, Z_X = 0, CONV_CH, 2 * CONV_CH
Z_Q = 3 * CONV_CH
Z_F = Z_Q + HG_WIDTH
Z_I = Z_F + HG_WIDTH
Z_G = Z_I + HG_WIDTH

HG_SUB = 16
HG_UNROLL = 2
VMEM_LIMIT = 56 * 1024 * 1024


def _cp(sem, vmem=VMEM_LIMIT):
    return pltpu.CompilerParams(dimension_semantics=sem, vmem_limit_bytes=vmem)


def _silu(x):
    return x * jax.nn.sigmoid(x)


def _rmsnorm(x, g):
    return x * lax.rsqrt(jnp.mean(x * x, axis=-1, keepdims=True) + EPS) * g


def _dot(a, b):
    return jnp.dot(a, b, preferred_element_type=F32)


def _dot_nt(a, b):
    return lax.dot_general(a, b, (((1,), (1,)), ((), ())), preferred_element_type=F32)


def _dot_tn(a, b):
    return lax.dot_general(a, b, (((0,), (0,)), ((), ())), preferred_element_type=F32)


def _ada_kernel(c_ref, w_ref, b_ref, o_ref):
    a = _silu(c_ref[...]).astype(BF16)
    o_ref[...] = _dot(a, w_ref[...].astype(BF16)) + b_ref[...]


def _ada_all(c_all, w_ada, b_ada, tn=1536):
    depth, d, n6 = w_ada.shape
    rows = c_all.shape[0]
    return pl.pallas_call(
        _ada_kernel,
        grid=(depth, n6 // tn),
        in_specs=[
            pl.BlockSpec((rows, d), lambda l, j: (0, 0)),
            pl.BlockSpec((None, d, tn), lambda l, j: (l, 0, j)),
            pl.BlockSpec((None, 1, tn), lambda l, j: (l, 0, j)),
        ],
        out_specs=pl.BlockSpec((None, rows, tn), lambda l, j: (l, 0, j)),
        out_shape=jax.ShapeDtypeStruct((depth, rows, n6), F32),
        compiler_params=_cp(("parallel", "parallel")),
        name="adaln_mod",
    )(c_all, w_ada, b_ada.reshape(depth, 1, n6))


class _Mod:
    def __init__(self, arr, tps):
        self.arr = arr
        self.tps = tps
        self.r = arr.shape[2]

    def spec(self, layer, col, width=D_MODEL, ncol=None):
        tps = self.tps
        per = D_MODEL // width
        if ncol is None:
            return pl.BlockSpec((None, None, self.r, width),
                                lambda i, *_: (layer, i // tps, 0, col * per))
        return pl.BlockSpec((None, None, self.r, width),
                            lambda i, j, *_: (layer, i // tps, 0, col * per + j))


def _resident(stacked, layer):
    shape = stacked.shape[1:]
    return pl.BlockSpec((None,) + shape, lambda *_: (layer,) + (0,) * len(shape),
                        pipeline_mode=pl.Buffered(1))


def _modmm_kernel(x_ref, g_ref, sh_ref, sc_ref, w_ref, o_ref):
    h = _rmsnorm(x_ref[...], g_ref[...]) * (1.0 + sc_ref[...]) + sh_ref[...]
    o_ref[...] = _dot(h.astype(BF16), w_ref[...])


def _mod_matmul(x, g, mod, layer, w_all, j, tm):
    m, d = x.shape
    n = w_all.shape[2]
    return pl.pallas_call(
        _modmm_kernel,
        grid=(m // tm,),
        in_specs=[
            pl.BlockSpec((tm, d), lambda i: (i, 0)),
            pl.BlockSpec((1, d), lambda i: (0, 0)),
            mod.spec(layer, 0),
            mod.spec(layer, 1),
            _resident(w_all, j),
        ],
        out_specs=pl.BlockSpec((tm, n), lambda i: (i, 0)),
        out_shape=jax.ShapeDtypeStruct((m, n), F32),
        compiler_params=_cp(("parallel",)),
        name="mod_matmul",
    )(x, g, mod.arr, mod.arr, w_all)


def _ffn_kernel(*refs, final_norm, dff, fc, n_mix):
    mix = refs[:n_mix]
    (x_ref, gmix_ref, g_ref, sh_ref, sc_ref, gate_ref, wgu_ref, wd_ref, gf_ref, o_ref,
     a_scr) = refs[n_mix:]
    if n_mix == 2:
        a_ref, wout_ref = mix
        y_mix = _dot(a_ref[...], wout_ref[...])
    else:
        ctx_ref, wuv_ref, wo_ref = mix
        parts = [_dot(ctx_ref[:, h * KV_LORA:(h + 1) * KV_LORA], wuv_ref[h]).astype(BF16)
                 for h in range(MLA_HEADS)]
        y_mix = _dot(jnp.concatenate(parts, axis=1), wo_ref[...])
    x = x_ref[...] + gmix_ref[...] * y_mix
    h = (_rmsnorm(x, g_ref[...]) * (1.0 + sc_ref[...]) + sh_ref[...]).astype(BF16)
    for c in range(dff // fc):
        gg = _dot(h, wgu_ref[:, c * fc:(c + 1) * fc])
        uu = _dot(h, wgu_ref[:, dff + c * fc:dff + (c + 1) * fc])
        a_scr[:, c * fc:(c + 1) * fc] = (_silu(gg) * uu).astype(BF16)
    y = x + gate_ref[...] * _dot(a_scr[...], wd_ref[...])
    if final_norm:
        y = _rmsnorm(y, gf_ref[...])
    o_ref[...] = y


def _ffn(mix, mix_specs, x, g, mod, layer, w_gu, w_down, g_final, final_norm, tm, fc):
    m, d = x.shape
    dff = w_down.shape[1]
    return pl.pallas_call(
        functools.partial(_ffn_kernel, final_norm=final_norm, dff=dff, fc=fc, n_mix=len(mix)),
        grid=(m // tm,),
        in_specs=list(mix_specs) + [
            pl.BlockSpec((tm, d), lambda i: (i, 0)),
            mod.spec(layer, 2),
            pl.BlockSpec((1, d), lambda i: (0, 0)),
            mod.spec(layer, 3),
            mod.spec(layer, 4),
            mod.spec(layer, 5),
            _resident(w_gu, layer),
            _resident(w_down, layer),
            pl.BlockSpec((1, d), lambda i: (0, 0)),
        ],
        out_specs=pl.BlockSpec((tm, d), lambda i: (i, 0)),
        out_shape=jax.ShapeDtypeStruct((m, d), F32),
        scratch_shapes=[pltpu.VMEM((tm, dff), BF16)],
        compiler_params=_cp(("parallel",)),
        name="mix_out_ffn",
    )(*mix, x, mod.arr, g, mod.arr, mod.arr, mod.arr, w_gu, w_down, g_final)


def _hg_lower_bound(lbl_ref, j):
    logits = lbl_ref[...]
    e = jnp.exp(logits - jnp.max(logits, axis=0, keepdims=True))
    den = jnp.sum(e, axis=0, keepdims=True)
    lb = jnp.zeros_like(den)
    for i in range(j):
        lb = lb + e[i:i + 1, :] / den
    return lb


def _cumsum_rows(x):
    rows = x.shape[0]
    idx = lax.broadcasted_iota(jnp.int32, x.shape, 0)
    d = 1
    while d < rows:
        x = x + jnp.where(idx >= d, pltpu.roll(x, d, 0), 0.0)
        d *= 2
    return x


def _hgrn_gates(qp, fp, lb):
    logf = jnp.log(lb + (1.0 - lb) * jax.nn.sigmoid(fp))
    kk = (1.0 - lb) * jax.nn.sigmoid(-fp)
    q = _silu(qp) * (HG_DK ** -0.5)
    return q, kk, logf


def _prompt_mixer_kernel(z_ref, cw_ref, lbl_ref, gn_ref, ab_ref, conv_ref, s_ref, ubuf, st_scr,
                         *, layer_j, tt, ns):
    t = pl.program_id(1)
    nt = pl.num_programs(1)

    @pl.when(t == 0)
    def _():
        ubuf[:, 0:8, :] = jnp.zeros((ns, 8, CONV_CH), F32)
        st_scr[...] = jnp.zeros_like(st_scr)

    for n in range(ns):
        u = z_ref[n, :, Z_C:Z_C + CONV_CH] * z_ref[n, :, Z_X:Z_X + CONV_CH]
        ubuf[n, 8:8 + tt, :] = u
        y = (cw_ref[0:1, :] * ubuf[n, 6:6 + tt, :] + cw_ref[1:2, :] * ubuf[n, 7:7 + tt, :]
             + cw_ref[2:3, :] * ubuf[n, 8:8 + tt, :])
        ab_ref[n, :, 0:CONV_CH] = (z_ref[n, :, Z_B:Z_B + CONV_CH] * y).astype(BF16)
        last2 = ubuf[n, tt + 6:tt + 8, :]
        ubuf[n, 6:8, :] = last2
        conv_ref[n] = last2

    lb_all = _hg_lower_bound(lbl_ref, layer_j)
    gn = gn_ref[...]
    ell = HG_SUB
    row8 = lax.broadcasted_iota(jnp.int32, (SUBLANES, HG_DK), 0)

    def chunk(c, carry):
        r0 = pl.multiple_of(c * ell, ell)
        for n in range(ns):
            for h in range(HG_HEADS):
                lo = h * HG_DK
                lb = lb_all[:, lo:lo + HG_DK]
                qp = z_ref[n, pl.ds(r0, ell), Z_Q + lo:Z_Q + lo + HG_DK]
                fp = z_ref[n, pl.ds(r0, ell), Z_F + lo:Z_F + lo + HG_DK]
                v = z_ref[n, pl.ds(r0, ell), Z_I + lo:Z_I + lo + HG_DK]
                gp = z_ref[n, pl.ds(r0, ell), Z_G + lo:Z_G + lo + HG_DK]
                q, kk, logf = _hgrn_gates(qp, fp, lb)
                gc = _cumsum_rows(logf)
                gl = gc[ell - 1:ell, :]
                st = st_scr[n, h]
                o = _dot_nt((q * jnp.exp(gc)).astype(BF16), st.astype(BF16))
                o_grp = [o[r:r + SUBLANES, :] for r in range(0, ell, SUBLANES)]
                gc2 = gc * LOG2E
                for s in range(ell):
                    for gi, r in enumerate(range(0, ell, SUBLANES)):
                        if r + SUBLANES <= s:
                            continue
                        d = gc2[r:r + SUBLANES, :] - gc2[s:s + 1, :]
                        if r <= s:
                            d = jnp.where(row8 >= s - r, d, NEG)
                        w = q[r:r + SUBLANES, :] * kk[s:s + 1, :] * jnp.exp2(d)
                        o_grp[gi] = (o_grp[gi]
                                     + jnp.sum(w, axis=-1, keepdims=True) * v[s:s + 1, :])
                o = jnp.concatenate(o_grp, axis=0)
                kd = kk * jnp.exp(gl - gc)
                st_scr[n, h] = st * jnp.exp(gl) + _dot_tn(v.astype(BF16), kd.astype(BF16))
                b = _rmsnorm(o, gn) * _silu(gp)
                ab_ref[n, pl.ds(r0, ell), CONV_CH + lo:CONV_CH + lo + HG_DK] = b.astype(BF16)
        return carry

    lax.fori_loop(0, tt // ell, chunk, 0, unroll=HG_UNROLL)

    @pl.when(t == nt - 1)
    def _():
        for n in range(ns):
            for h in range(HG_HEADS):
                s_ref[n, h] = st_scr[n, h].T


def _prompt_mixer(z, conv_w_j, lb_logits, gnorm_j, layer_j, nb, seq, tt, ns):
    nt = seq // tt
    return pl.pallas_call(
        functools.partial(_prompt_mixer_kernel, layer_j=layer_j, tt=tt, ns=ns),
        grid=(nb // ns, nt),
        in_specs=[
            pl.BlockSpec((ns, tt, EVEN_IN), lambda g, t: (g, t, 0)),
            pl.BlockSpec((CONV_W, CONV_CH), lambda g, t: (0, 0)),
            pl.BlockSpec(lb_logits.shape, lambda g, t: (0, 0)),
            pl.BlockSpec((1, HG_DK), lambda g, t: (0, 0)),
        ],
        out_specs=[
            pl.BlockSpec((ns, tt, D_MODEL), lambda g, t: (g, t, 0)),
            pl.BlockSpec((ns, CONV_W - 1, CONV_CH), lambda g, t: (g, 0, 0)),
            pl.BlockSpec((ns, HG_HEADS, HG_DK, HG_DK), lambda g, t: (g, 0, 0, 0)),
        ],
        out_shape=[
            jax.ShapeDtypeStruct((nb, seq, D_MODEL), BF16),
            jax.ShapeDtypeStruct((nb, CONV_W - 1, CONV_CH), F32),
            jax.ShapeDtypeStruct((nb, HG_HEADS, HG_DK, HG_DK), F32),
        ],
        scratch_shapes=[pltpu.VMEM((ns, tt + 8, CONV_CH), F32),
                        pltpu.VMEM((ns, HG_HEADS, HG_DK, HG_DK), F32)],
        compiler_params=_cp(("parallel", "arbitrary")),
        name="prompt_conv_hgrn",
    )(z, conv_w_j, lb_logits, gnorm_j)


def _column(row_vec, eye):
    return jnp.sum(jnp.where(eye, row_vec, 0.0), axis=1, keepdims=True)


def _sample_mixer_kernel(z_ref, cb_ref, s0_ref, cw_ref, lbl_ref, gn_ref, ab_ref, conv_ref, s_ref,
                         b_scr, *, layer_j, tb):
    u = z_ref[:, Z_C:Z_C + CONV_CH] * z_ref[:, Z_X:Z_X + CONV_CH]
    b0 = cb_ref[:, 0, :]
    b1 = cb_ref[:, 1, :]
    y = cw_ref[0:1, :] * b0 + cw_ref[1:2, :] * b1 + cw_ref[2:3, :] * u
    ab_ref[:, 0:CONV_CH] = (z_ref[:, Z_B:Z_B + CONV_CH] * y).astype(BF16)
    conv_ref[:, 0, :] = b1
    conv_ref[:, 1, :] = u

    lb_all = _hg_lower_bound(lbl_ref, layer_j)
    gn = gn_ref[...]
    eye = (lax.broadcasted_iota(jnp.int32, (HG_DK, HG_DK), 0)
           == lax.broadcasted_iota(jnp.int32, (HG_DK, HG_DK), 1))

    q_all, kk_all, logf_all = _hgrn_gates(z_ref[:, Z_Q:Z_Q + HG_WIDTH], z_ref[:, Z_F:Z_F + HG_WIDTH],
                                          lb_all)
    ef_all = jnp.exp(logf_all)
    v_all = z_ref[:, Z_I:Z_I + HG_WIDTH]
    for n in range(tb):
        for h in range(HG_HEADS):
            lo = h * HG_DK
            row = lambda a: a[n:n + 1, lo:lo + HG_DK]
            s_new = (_column(row(ef_all), eye) * s0_ref[n, h]
                     + _column(row(kk_all), eye) * row(v_all))
            s_ref[n, h] = s_new
            b_scr[n:n + 1, lo:lo + HG_DK] = jnp.sum(_column(row(q_all), eye) * s_new, axis=0,
                                                    keepdims=True)
    for h in range(HG_HEADS):
        lo = h * HG_DK
        b = _rmsnorm(b_scr[:, lo:lo + HG_DK], gn) * _silu(z_ref[:, Z_G + lo:Z_G + lo + HG_DK])
        ab_ref[:, CONV_CH + lo:CONV_CH + lo + HG_DK] = b.astype(BF16)


def _sample_mixer(z, conv_buf, s0, conv_w_j, lb_logits, gnorm_j, layer_j, tb):
    nb = z.shape[0]
    return pl.pallas_call(
        functools.partial(_sample_mixer_kernel, layer_j=layer_j, tb=tb),
        grid=(nb // tb,),
        in_specs=[
            pl.BlockSpec((tb, EVEN_IN), lambda i: (i, 0)),
            pl.BlockSpec((None, tb, CONV_W - 1, CONV_CH), lambda i: (layer_j, i, 0, 0)),
            pl.BlockSpec((None, tb, HG_HEADS, HG_DK, HG_DK), lambda i: (layer_j, i, 0, 0, 0)),
            pl.BlockSpec((CONV_W, CONV_CH), lambda i: (0, 0)),
            pl.BlockSpec(lb_logits.shape, lambda i: (0, 0)),
            pl.BlockSpec((1, HG_DK), lambda i: (0, 0)),
        ],
        out_specs=[
            pl.BlockSpec((tb, D_MODEL), lambda i: (i, 0)),
            pl.BlockSpec((tb, CONV_W - 1, CONV_CH), lambda i: (i, 0, 0)),
            pl.BlockSpec((tb, HG_HEADS, HG_DK, HG_DK), lambda i: (i, 0, 0, 0)),
        ],
        out_shape=[
            jax.ShapeDtypeStruct((nb, D_MODEL), BF16),
            jax.ShapeDtypeStruct((nb, CONV_W - 1, CONV_CH), F32),
            jax.ShapeDtypeStruct((nb, HG_HEADS, HG_DK, HG_DK), F32),
        ],
        scratch_shapes=[pltpu.VMEM((tb, HG_WIDTH), F32)],
        compiler_params=_cp(("parallel",)),
        name="sample_conv_hgrn",
    )(z, conv_buf, s0, conv_w_j, lb_logits, gnorm_j)


def _rope_table_kernel(cos_ref, sin_ref, *, tr, pos0, step):
    lane = lax.broadcasted_iota(jnp.int32, (tr, LANES), 1)
    rowi = lax.broadcasted_iota(jnp.int32, (tr, LANES), 0)
    half = ROPE_D // 2
    fi = (lane % half).astype(F32)
    inv = jnp.exp(fi * (-math.log(ROPE_THETA) / half))
    pos = (pos0 + step * (pl.program_id(0) * tr + rowi)).astype(F32)
    ang = pos * inv
    sign = jnp.where((lane % ROPE_D) < half, -1.0, 1.0)
    cos_ref[...] = jnp.cos(ang)
    sin_ref[...] = jnp.sin(ang) * sign


def _rope_table(rows, pos0, step, tr):
    return pl.pallas_call(
        functools.partial(_rope_table_kernel, tr=tr, pos0=pos0, step=step),
        grid=(rows // tr,),
        out_specs=[pl.BlockSpec((tr, LANES), lambda i: (i, 0))] * 2,
        out_shape=[jax.ShapeDtypeStruct((rows, LANES), F32)] * 2,
        compiler_params=_cp(("parallel",)),
        name="rope_table",
    )()


def _rope_pairs(g, cos, sin_signed):
    lane = lax.broadcasted_iota(jnp.int32, g.shape, 1)
    half = ROPE_D // 2
    n = g.shape[1]
    rot = jnp.where((lane % ROPE_D) < half, pltpu.roll(g, n - half, 1), pltpu.roll(g, half, 1))
    return g * cos + rot * sin_signed


def _dqkv_kernel(x_ref, g_ref, sh_ref, sc_ref, w_ref, qn_ref, kvn_ref, cos_ref, sin_ref, *rest,
                 per_head_kv):
    if per_head_kv:
        wuk_ref, wuv_ref, cq_ref, ckv_ref, kr_ref, kc_ref, v_ref = rest
    else:
        cq_ref, ckv_ref, kr_ref = rest
    tm = x_ref.shape[0]
    sub = min(tm, 256)
    for r in range(0, tm, sub):
        rows = slice(r, r + sub)
        sc = sc_ref[...] if sc_ref.shape[0] == 1 else sc_ref[rows, :]
        sh = sh_ref[...] if sh_ref.shape[0] == 1 else sh_ref[rows, :]
        h = _rmsnorm(x_ref[rows, :], g_ref[...]) * (1.0 + sc) + sh
        d = _dot(h.astype(BF16), w_ref[...])
        cq_ref[rows, :] = _rmsnorm(d[:, :Q_LORA], qn_ref[...]).astype(BF16)
        ckv = _rmsnorm(d[:, Q_LORA:Q_LORA + KV_LORA], kvn_ref[...])
        ckv_ref[rows, :] = ckv
        kr = _rope_pairs(d[:, Q_LORA + KV_LORA:], cos_ref[rows, :], sin_ref[rows, :])[:, :ROPE_D]
        kr_ref[rows, :] = kr
        if per_head_kv:
            ckv_b = ckv.astype(BF16)
            k_all = _dot(ckv_b, wuk_ref[...]).astype(BF16)
            v_all = _dot(ckv_b, wuv_ref[...]).astype(BF16)
            tail = kr.astype(BF16)
            for hd in range(MLA_HEADS):
                kc_ref[hd, rows, 0:NOPE] = k_all[:, hd * NOPE:(hd + 1) * NOPE]
                kc_ref[hd, rows, NOPE:] = tail
                v_ref[hd, rows, :] = v_all[:, hd * V_D:(hd + 1) * V_D]


def _dqkv(x, g, mod, layer, w_pad, q_norm, kv_norm, cos_t, sin_t, tm, rope_blocks, seq=None,
          w_uk_flat=None, w_uv_flat=None):
    m, d = x.shape
    n = w_pad.shape[1]
    rb = rope_blocks
    per_head_kv = w_uk_flat is not None
    in_specs = [
        pl.BlockSpec((tm, d), lambda i: (i, 0)),
        pl.BlockSpec((1, d), lambda i: (0, 0)),
        mod.spec(layer, 0),
        mod.spec(layer, 1),
        pl.BlockSpec((d, n), lambda i: (0, 0)),
        pl.BlockSpec((1, Q_LORA), lambda i: (0, 0)),
        pl.BlockSpec((1, KV_LORA), lambda i: (0, 0)),
        pl.BlockSpec((cos_t.shape[0] // rb, LANES), lambda i: (i % rb, 0)),
        pl.BlockSpec((cos_t.shape[0] // rb, LANES), lambda i: (i % rb, 0)),
    ]
    out_specs = [
        pl.BlockSpec((tm, Q_LORA), lambda i: (i, 0)),
        pl.BlockSpec((tm, KV_LORA), lambda i: (i, 0)),
        pl.BlockSpec((tm, ROPE_D), lambda i: (i, 0)),
    ]
    out_shape = [
        jax.ShapeDtypeStruct((m, Q_LORA), BF16),
        jax.ShapeDtypeStruct((m, KV_LORA), F32),
        jax.ShapeDtypeStruct((m, ROPE_D), F32),
    ]
    args = [x, g, mod.arr, mod.arr, w_pad, q_norm, kv_norm, cos_t, sin_t]
    if per_head_kv:
        tps = seq // tm
        nb = m // seq
        in_specs += [pl.BlockSpec(w_uk_flat.shape, lambda i: (0, 0)),
                     pl.BlockSpec(w_uv_flat.shape, lambda i: (0, 0))]
        out_specs += [
            pl.BlockSpec((None, MLA_HEADS, tm, HEAD_QK), lambda i: (i // tps, 0, i % tps, 0)),
            pl.BlockSpec((None, MLA_HEADS, tm, V_D), lambda i: (i // tps, 0, i % tps, 0)),
        ]
        out_shape += [
            jax.ShapeDtypeStruct((nb, MLA_HEADS, seq, HEAD_QK), BF16),
            jax.ShapeDtypeStruct((nb, MLA_HEADS, seq, V_D), BF16),
        ]
        args += [w_uk_flat, w_uv_flat]
    return pl.pallas_call(
        functools.partial(_dqkv_kernel, per_head_kv=per_head_kv),
        grid=(m // tm,),
        in_specs=in_specs,
        out_specs=out_specs,
        out_shape=out_shape,
        compiler_params=_cp(("parallel",)),
        name="mla_down_proj",
    )(*args)


def _q_kernel(cq_ref, wn_ref, wr_ref, cos_ref, sin_ref, *rest, absorb):
    cq = cq_ref[...]
    qn = _dot(cq, wn_ref[...])
    qr = _dot(cq, wr_ref[...])
    cos = jnp.concatenate([cos_ref[...]] * (MLA_HEADS * ROPE_D // LANES), axis=1)
    sin = jnp.concatenate([sin_ref[...]] * (MLA_HEADS * ROPE_D // LANES), axis=1)
    qr = _rope_pairs(qr, cos, sin).astype(BF16)
    if absorb:
        wuk_ref, ql_ref, qr_ref = rest
        for h in range(MLA_HEADS):
            ql_ref[h] = _dot(qn[:, h * NOPE:(h + 1) * NOPE].astype(BF16), wuk_ref[h]).astype(BF16)
            qr_ref[h] = qr[:, h * ROPE_D:(h + 1) * ROPE_D]
    else:
        (qc_ref,) = rest
        qn = qn.astype(BF16)
        for h in range(MLA_HEADS):
            qc_ref[h, :, 0:NOPE] = qn[:, h * NOPE:(h + 1) * NOPE]
            qc_ref[h, :, NOPE:] = qr[:, h * ROPE_D:(h + 1) * ROPE_D]


def _q_proj(cq, w_nope, w_rope, w_ukt, cos_t, sin_t, nb, seq, tm, rope_blocks):
    nt = seq // tm
    rb = rope_blocks
    absorb = w_ukt is not None
    in_specs = [
        pl.BlockSpec((tm, Q_LORA), lambda n, t: (n * nt + t, 0)),
        pl.BlockSpec(w_nope.shape, lambda n, t: (0, 0)),
        pl.BlockSpec(w_rope.shape, lambda n, t: (0, 0)),
        pl.BlockSpec((cos_t.shape[0] // rb, LANES), lambda n, t: (t % rb, 0)),
        pl.BlockSpec((cos_t.shape[0] // rb, LANES), lambda n, t: (t % rb, 0)),
    ]
    args = [cq, w_nope, w_rope, cos_t, sin_t]
    if absorb:
        in_specs.append(pl.BlockSpec(w_ukt.shape, lambda n, t: (0, 0, 0)))
        args.append(w_ukt)
        widths = (KV_LORA, ROPE_D)
    else:
        widths = (HEAD_QK,)
    return pl.pallas_call(
        functools.partial(_q_kernel, absorb=absorb),
        grid=(nb, nt),
        in_specs=in_specs,
        out_specs=[pl.BlockSpec((None, MLA_HEADS, tm, w), lambda n, t: (n, 0, t, 0)) for w in widths],
        out_shape=[jax.ShapeDtypeStruct((nb, MLA_HEADS, seq, w), BF16) for w in widths],
        compiler_params=_cp(("parallel", "parallel")),
        name="mla_q_proj",
    )(*args)


def _lane_tile(x, width):
    return x if width == LANES else jnp.concatenate([x] * (width // LANES), axis=1)


def _prompt_attn_kernel(q_ref, k_ref, v_ref, o_ref, m_scr, l_scr, acc_scr, *, tq):
    i = pl.program_id(1)
    m_scr[...] = jnp.full_like(m_scr, NEG)
    l_scr[...] = jnp.zeros_like(l_scr)
    acc_scr[...] = jnp.zeros_like(acc_scr)

    def block(j, masked):
        r0 = pl.multiple_of(j * tq, tq)
        if masked:
            causal = (lax.broadcasted_iota(jnp.int32, (tq, tq), 1)
                      <= lax.broadcasted_iota(jnp.int32, (tq, tq), 0))

        def head(h, carry):
            s = _dot_nt(q_ref[h], k_ref[h, pl.ds(r0, tq), :]) * (MLA_SCALE * LOG2E)
            if masked:
                s = jnp.where(causal, s, NEG)
            m_prev = m_scr[h]
            m_new = jnp.maximum(m_prev, jnp.max(s, axis=-1, keepdims=True))
            alpha = jnp.exp2(m_prev - m_new)
            p = jnp.exp2(s - _lane_tile(m_new, tq))
            l_scr[h] = alpha * l_scr[h] + jnp.sum(p, axis=-1, keepdims=True)
            acc_scr[h] = alpha * acc_scr[h] + _dot(p.astype(BF16), v_ref[h, pl.ds(r0, tq), :])
            m_scr[h] = m_new
            return carry

        lax.fori_loop(0, MLA_HEADS, head, 0, unroll=True)

    def body(j, carry):
        block(j, False)
        return carry

    lax.fori_loop(0, i, body, 0)
    block(i, True)
    for h in range(MLA_HEADS):
        o_ref[:, h * V_D:(h + 1) * V_D] = (acc_scr[h] / l_scr[h]).astype(BF16)


def _prompt_attn(q_cat, k_cat, v_heads, nb, seq, tq):
    nq = seq // tq
    return pl.pallas_call(
        functools.partial(_prompt_attn_kernel, tq=tq),
        grid=(nb, nq),
        in_specs=[
            pl.BlockSpec((None, MLA_HEADS, tq, HEAD_QK), lambda n, i: (n, 0, i, 0)),
            pl.BlockSpec((None, MLA_HEADS, seq, HEAD_QK), lambda n, i: (n, 0, 0, 0)),
            pl.BlockSpec((None, MLA_HEADS, seq, V_D), lambda n, i: (n, 0, 0, 0)),
        ],
        out_specs=pl.BlockSpec((tq, MLA_HEADS * V_D), lambda n, i: (n * nq + i, 0)),
        out_shape=jax.ShapeDtypeStruct((nb * seq, MLA_HEADS * V_D), BF16),
        scratch_shapes=[pltpu.VMEM((MLA_HEADS, tq, LANES), F32),
                        pltpu.VMEM((MLA_HEADS, tq, LANES), F32),
                        pltpu.VMEM((MLA_HEADS, tq, V_D), F32)],
        compiler_params=_cp(("parallel", "arbitrary")),
        name="mla_prompt_attn",
    )(q_cat, k_cat, v_heads)


def _sample_attn_kernel(pt_ref, ql_ref, qr_ref, cn_ref, krn_ref, ckv_hbm, krt_hbm, o_ref,
                        kbuf, rbuf, kb16, kt16, sems, *, layer_j, n_pages):
    b = pl.program_id(0)
    nb = pl.num_programs(0)
    slot = b % 2

    def page_copies(seq, slot_, p):
        page = pt_ref[seq, p]
        return (
            pltpu.make_async_copy(ckv_hbm.at[layer_j, page],
                                  kbuf.at[slot_, p * PAGE_SIZE:(p + 1) * PAGE_SIZE, :],
                                  sems.at[0, slot_]),
            pltpu.make_async_copy(krt_hbm.at[layer_j, page],
                                  rbuf.at[slot_, :, p * PAGE_SIZE:(p + 1) * PAGE_SIZE],
                                  sems.at[1, slot_]),
        )

    def start_all(seq, slot_):
        for p in range(n_pages):
            for cp in page_copies(seq, slot_, p):
                cp.start()

    @pl.when(b == 0)
    def _():
        start_all(0, 0)

    @pl.when(b + 1 < nb)
    def _():
        start_all(b + 1, 1 - slot)

    for p in range(n_pages):
        for cp in page_copies(b, slot, p):
            cp.wait()

    ql = ql_ref[...]
    qr = qr_ref[...]
    past = n_pages * PAGE_SIZE
    tc = min(past, 1024)
    for c in range(past // tc):
        kb = kbuf[slot, c * tc:(c + 1) * tc, :].astype(BF16)
        kb16[c * tc:(c + 1) * tc, :] = kb
        kt16[:, c * tc:(c + 1) * tc] = kb.T
    s = (_dot(ql, kt16[...]) + _dot(qr, rbuf[slot].astype(BF16))) * MLA_SCALE
    cn = cn_ref[...]
    s_new = (jnp.sum(ql.astype(F32) * cn, axis=-1, keepdims=True)
             + jnp.sum(qr.astype(F32) * krn_ref[...], axis=-1, keepdims=True)) * MLA_SCALE
    m = jnp.maximum(jnp.max(s, axis=-1, keepdims=True), s_new)
    p = jnp.exp(s - m)
    p_new = jnp.exp(s_new - m)
    l = jnp.sum(p, axis=-1, keepdims=True) + p_new
    acc = _dot(p.astype(BF16), kb16[...]) + p_new * cn
    o_ref[...] = (acc / l).astype(BF16)


def _sample_attn(page_table, q_lat, q_rope, ckv_new, kr_new, cache_ckv, cache_krope_t, layer_j):
    nb, n_pages = page_table.shape
    past = n_pages * PAGE_SIZE
    grid_spec = pltpu.PrefetchScalarGridSpec(
        num_scalar_prefetch=1,
        grid=(nb,),
        in_specs=[
            pl.BlockSpec((None, MLA_HEADS, KV_LORA), lambda b, pt: (b, 0, 0)),
            pl.BlockSpec((None, MLA_HEADS, ROPE_D), lambda b, pt: (b, 0, 0)),
            pl.BlockSpec((None, 1, KV_LORA), lambda b, pt: (b, 0, 0)),
            pl.BlockSpec((None, 1, ROPE_D), lambda b, pt: (b, 0, 0)),
            pl.BlockSpec(memory_space=pl.ANY),
            pl.BlockSpec(memory_space=pl.ANY),
        ],
        out_specs=pl.BlockSpec((None, MLA_HEADS, KV_LORA), lambda b, pt: (b, 0, 0)),
        scratch_shapes=[
            pltpu.VMEM((2, past, KV_LORA), F32),
            pltpu.VMEM((2, ROPE_D, past), F32),
            pltpu.VMEM((past, KV_LORA), BF16),
            pltpu.VMEM((KV_LORA, past), BF16),
            pltpu.SemaphoreType.DMA((2, 2)),
        ],
    )
    return pl.pallas_call(
        functools.partial(_sample_attn_kernel, layer_j=layer_j, n_pages=n_pages),
        grid_spec=grid_spec,
        out_shape=jax.ShapeDtypeStruct((nb, MLA_HEADS, KV_LORA), BF16),
        compiler_params=_cp(("arbitrary",)),
        name="mla_sample_attn",
    )(page_table, q_lat, q_rope, ckv_new.reshape(nb, 1, KV_LORA), kr_new.reshape(nb, 1, ROPE_D),
      cache_ckv, cache_krope_t)


def _tile(m, pref):
    t = min(m, pref)
    assert m % t == 0, (m, t)
    return t


def kernel(x_prompt, x_sample, c_prompt, c_sample, cache_conv, state_hgrn, cache_ckv, cache_krope,
           page_table, w_ada, b_ada, norm_mix, norm_ffn, w_in_even, conv_w, hg_lb_logits, hg_gnorm,
           w_out_even, w_dqkv, q_norm, w_uq, kv_norm, w_uk, w_uv, w_o, w_gu, w_down, norm_final):
    bp, tp, d = x_prompt.shape
    bs, ts, _ = x_sample.shape
    assert ts == 1 and d == D_MODEL
    depth = w_ada.shape[0]
    n_pages = page_table.shape[1]
    past_len = n_pages * PAGE_SIZE
    mp, ms = bp * tp, bs

    tm_p = _tile(tp, 512)
    tm_s = ms
    tt = _tile(tp, 256)
    tq = _tile(tp, 512)
    tb = _tile(bs, 16)
    ns = 1
    krope_t = jnp.swapaxes(cache_krope, 2, 3)
    fc = 256

    mods = _ada_all(jnp.concatenate([c_prompt, c_sample], axis=0), w_ada, b_ada)
    mod_p = _Mod(mods[:, :bp].reshape(depth, bp, 1, N_MOD * d), tp // tm_p)
    mod_s = _Mod(mods[:, bp:].reshape(depth, 1, bs, N_MOD * d), 1)
    tm_f = _tile(tp, 1024)
    mod_pf = _Mod(mod_p.arr, tp // tm_f)

    cos_p, sin_p = _rope_table(tp, 0, 1, _tile(tp, 512))
    cos_s, sin_s = _rope_table(8, past_len, 0, 8)
    cos_s = jnp.broadcast_to(cos_s[:1], (ms, LANES))
    sin_s = jnp.broadcast_to(sin_s[:1], (ms, LANES))

    w_in_b = w_in_even.astype(BF16)
    w_out_b = w_out_even.astype(BF16)
    w_gu_b = w_gu.astype(BF16)
    w_down_b = w_down.astype(BF16)
    w_o_b = w_o.astype(BF16)
    w_uvh_b = jnp.transpose(w_uv, (0, 2, 1, 3)).astype(BF16)

    xp = x_prompt.reshape(mp, d)
    xs = x_sample.reshape(ms, d)
    outs ={k: [] for k in ("conv_p", "hg_p", "ckv_p", "kr_p", "conv_s", "hg_s", "ckv_s", "kr_s")}

    for l in range(depth):
        j = l // 2
        g_mix = norm_mix[l].reshape(1, d)
        g_ffn = norm_ffn[l].reshape(1, d)
        if l % 2 == 0:
            gn = hg_gnorm[j].reshape(1, HG_DK)
            zp = _mod_matmul(xp, g_mix, mod_pf, l, w_in_b, j, tm_f)
            zs = _mod_matmul(xs, g_mix, mod_s, l, w_in_b, j, tm_s)
            abp, cvp, hgp = _prompt_mixer(zp.reshape(bp, tp, EVEN_IN), conv_w[j], hg_lb_logits, gn,
                                          j, bp, tp, tt, ns)
            abp = abp.reshape(mp, d)
            abs_, cvs, hgs = _sample_mixer(zs, cache_conv, state_hgrn, conv_w[j], hg_lb_logits,
                                           gn, j, tb)
            mix_p, mix_s = (abp, w_out_b), (abs_, w_out_b)
            mix_w_specs_p = mix_w_specs = [_resident(w_out_b, j)]
            outs["conv_p"].append(cvp)
            outs["hg_p"].append(hgp)
            outs["conv_s"].append(cvs)
            outs["hg_s"].append(hgs)
        else:
            w_d = jnp.pad(w_dqkv[j], ((0, 0), (0, ROPE_D))).astype(BF16)
            wq = w_uq[j].reshape(Q_LORA, MLA_HEADS, NOPE + ROPE_D)
            w_nope = wq[:, :, :NOPE].reshape(Q_LORA, MLA_HEADS * NOPE).astype(BF16)
            w_rope = wq[:, :, NOPE:].reshape(Q_LORA, MLA_HEADS * ROPE_D).astype(BF16)
            w_ukt = jnp.transpose(w_uk[j], (1, 2, 0)).astype(BF16)
            qn = q_norm[j].reshape(1, Q_LORA)
            kvn = kv_norm[j].reshape(1, KV_LORA)

            w_uk_flat = w_uk[j].reshape(KV_LORA, MLA_HEADS * NOPE).astype(BF16)
            w_uv_flat = w_uv[j].reshape(KV_LORA, MLA_HEADS * V_D).astype(BF16)
            cqp, ckvp, krp, kcp, vhp = _dqkv(xp, g_mix, mod_p, l, w_d, qn, kvn, cos_p, sin_p, tm_p,
                                             tp // tm_p, tp, w_uk_flat, w_uv_flat)
            cqs, ckvs, krs = _dqkv(xs, g_mix, mod_s, l, w_d, qn, kvn, cos_s, sin_s, tm_s, 1)
            (qcp,) = _q_proj(cqp, w_nope, w_rope, None, cos_p, sin_p, bp, tp, tm_p, tp // tm_p)
            qls, qrs = _q_proj(cqs, w_nope, w_rope, w_ukt, cos_s, sin_s, 1, ms, tm_s, 1)
            ctxp = _prompt_attn(qcp, kcp, vhp, bp, tp, tq)
            ctxs = _sample_attn(page_table,
                                jnp.transpose(qls[0], (1, 0, 2)), jnp.transpose(qrs[0], (1, 0, 2)),
                                ckvs, krs, cache_ckv, krope_t, j)
            mix_p = (ctxp, w_o_b)
            mix_s = (ctxs.reshape(ms, MLA_HEADS * KV_LORA), w_uvh_b, w_o_b)
            mix_w_specs_p = [_resident(w_o_b, j)]
            mix_w_specs = [_resident(w_uvh_b, j), _resident(w_o_b, j)]
            outs["ckv_p"].append(ckvp.reshape(bp, tp, KV_LORA))
            outs["kr_p"].append(krp.reshape(bp, tp, ROPE_D))
            outs["ckv_s"].append(ckvs.reshape(bs, ts, KV_LORA))
            outs["kr_s"].append(krs.reshape(bs, ts, ROPE_D))

        last = l == depth - 1
        gfin = norm_final.reshape(1, d)
        def row_spec(a, tm):
            return pl.BlockSpec((tm, a.shape[1]), lambda i: (i, 0))

        xp = _ffn(mix_p, [row_spec(mix_p[0], tm_f)] + mix_w_specs_p, xp, g_ffn, mod_pf, l,
                  w_gu_b, w_down_b, gfin, last, tm_f, fc)
        xs = _ffn(mix_s, [row_spec(mix_s[0], tm_s)] + mix_w_specs, xs, g_ffn, mod_s, l,
                  w_gu_b, w_down_b, gfin, last, tm_s, fc)

    return (xp.reshape(bp, tp, d), xs.reshape(bs, ts, d),
            jnp.stack(outs["conv_p"]), jnp.stack(outs["hg_p"]),
            jnp.stack(outs["ckv_p"]), jnp.stack(outs["kr_p"]),
            jnp.stack(outs["conv_s"]), jnp.stack(outs["hg_s"]),
            jnp.stack(outs["ckv_s"]), jnp.stack(outs["kr_s"]))
```
